```python
import math
import jax, jax.numpy as jnp
from jax import lax
import numpy as np

D_MODEL = 1024
BATCH = 8
SEQ = 4096
DEPTH = 4
DEC_BATCH = 2
DEC_SEQ = 16384
PAST_LEN = 128

N_META = 16
GRID_W = 64
EPS = 1e-6
NEG = -1e30
S5_WIDTH = 256
S5_GROUP = 16
S5_GROUPS = S5_WIDTH // S5_GROUP
S5_STATE = 64
NA_HEADS = 8
NA_HEAD_DIM = 64
NA_WIDTH = NA_HEADS * NA_HEAD_DIM
NA_ROWS = 8
NA_COLS = 16
NA_QBLOCK = 16
NA_KBLOCK = NA_QBLOCK + NA_COLS
HG_HEADS = 4
HG_DK = 64
HG_DV = 64
HG_WIDTH = HG_HEADS * HG_DK
HG_CHUNK = 64
FFN_HIDDEN = -(-8 * D_MODEL // (3 * 256)) * 256
IN_SIZES = [S5_WIDTH, NA_WIDTH, NA_WIDTH, NA_WIDTH, HG_WIDTH, HG_WIDTH, HG_WIDTH, HG_WIDTH, HG_WIDTH, D_MODEL, D_MODEL, D_MODEL]
IN_WIDTH = sum(IN_SIZES)

kernel_name = "hybrid_s5_natten_hgrn2_encoder"


def rms_norm(x, g):
    xf = x.astype(jnp.float32)
    y = xf * lax.rsqrt(jnp.mean(xf * xf, axis=-1, keepdims=True) + EPS)
    return (y * g.astype(jnp.float32)).astype(x.dtype)


def s5_scan_dir(u, a_re, a_im, log_dt, b_re, b_im, c_re, c_im, reverse):
    dt = jnp.exp(log_dt)[:, None]
    mag = jnp.exp(a_re * dt)
    lbar_re = mag * jnp.cos(a_im * dt)
    lbar_im = mag * jnp.sin(a_im * dt)
    den = a_re * a_re + a_im * a_im
    nr = lbar_re - 1.0
    ni = lbar_im
    z_re = (nr * a_re + ni * a_im) / den
    z_im = (ni * a_re - nr * a_im) / den
    bbar_re = z_re[..., None] * b_re - z_im[..., None] * b_im
    bbar_im = z_re[..., None] * b_im + z_im[..., None] * b_re
    bu_re = jnp.einsum('blgc,gpc->blgp', u, bbar_re)
    bu_im = jnp.einsum('blgc,gpc->blgp', u, bbar_im)
    lr = jnp.broadcast_to(lbar_re, bu_re.shape)
    li = jnp.broadcast_to(lbar_im, bu_re.shape)

    def combine(e1, e2):
        a1r, a1i, b1r, b1i = e1
        a2r, a2i, b2r, b2i = e2
        return (a1r * a2r - a1i * a2i, a1r * a2i + a1i * a2r,
                a2r * b1r - a2i * b1i + b2r, a2r * b1i + a2i * b1r + b2i)

    _, _, xr, xi = lax.associative_scan(combine, (lr, li, bu_re, bu_im), axis=1, reverse=reverse)
    return jnp.einsum('blgp,gcp->blgc', xr, c_re) - jnp.einsum('blgp,gcp->blgc', xi, c_im)


def s5_mixer(u, a_re, a_im, log_dt, b_re, b_im, c_re, c_im, d_skip, w_glu):
    f32 = jnp.float32
    bsz, L, _ = u.shape
    uf = u.astype(f32)
    ug = uf.reshape(bsz, L, S5_GROUPS, S5_GROUP)
    y = (s5_scan_dir(ug, a_re[0].astype(f32), a_im[0].astype(f32), log_dt[0].astype(f32), b_re[0].astype(f32),
                     b_im[0].astype(f32), c_re[0].astype(f32), c_im[0].astype(f32), False)
         + s5_scan_dir(ug, a_re[1].astype(f32), a_im[1].astype(f32), log_dt[1].astype(f32), b_re[1].astype(f32),
                       b_im[1].astype(f32), c_re[1].astype(f32), c_im[1].astype(f32), True))
    y = y.reshape(bsz, L, S5_WIDTH) + d_skip.astype(f32) * uf
    y = jax.nn.gelu(y)
    y = y * jax.nn.sigmoid(y @ w_glu.astype(f32))
    return y.astype(u.dtype)


def neighborhood_attention(q, k, v, rpb):
    bsz, L, _ = q.shape
    n_tok = L - N_META
    rows = n_tok // GRID_W
    kr = min(NA_ROWS, rows)
    scale = NA_HEAD_DIM ** -0.5
    q = q.reshape(bsz, L, NA_HEADS, NA_HEAD_DIM) * scale
    k = k.reshape(bsz, L, NA_HEADS, NA_HEAD_DIM)
    v = v.reshape(bsz, L, NA_HEADS, NA_HEAD_DIM)
    qm, km, vm = q[:, :N_META], k[:, :N_META], v[:, :N_META]
    sm = jnp.einsum('bqhd,bkhd->bhqk', qm, km).astype(jnp.float32)
    om = jnp.einsum('bhqk,bkhd->bqhd', jax.nn.softmax(sm, axis=-1).astype(v.dtype), vm)
    grid = lambda t: t[:, N_META:].reshape(bsz, rows, GRID_W, NA_HEADS, NA_HEAD_DIM)
    qg, kg, vg = grid(q), grid(k), grid(v)
    ncb = GRID_W // NA_QBLOCK
    qcols = np.arange(GRID_W).reshape(ncb, NA_QBLOCK)
    ks = np.clip(np.arange(ncb) * NA_QBLOCK - NA_COLS // 2, 0, GRID_W - NA_KBLOCK)
    kcols = ks[:, None] + np.arange(NA_KBLOCK)
    wstart = np.clip(qcols - NA_COLS // 2, 0, GRID_W - NA_COLS)
    colmask = (kcols[:, None, :] >= wstart[..., None]) & (kcols[:, None, :] < wstart[..., None] + NA_COLS)
    dcol = np.clip(kcols[:, None, :] - qcols[..., None], -(NA_COLS - 1), NA_COLS - 1) + NA_COLS - 1
    mask = np.broadcast_to(colmask[:, :, None, :], (ncb, NA_QBLOCK, kr, NA_KBLOCK)).reshape(ncb, NA_QBLOCK, kr * NA_KBLOCK)
    rpb = rpb.astype(jnp.float32)

    def row_fn(args):
        r, q_row = args
        rs = jnp.clip(r - NA_ROWS // 2, 0, rows - kr)
        k_strip = lax.dynamic_slice_in_dim(kg, rs, kr, axis=1)
        v_strip = lax.dynamic_slice_in_dim(vg, rs, kr, axis=1)
        k_blk = k_strip[:, :, kcols].transpose(0, 2, 1, 3, 4, 5).reshape(bsz, ncb, kr * NA_KBLOCK, NA_HEADS, NA_HEAD_DIM)
        v_blk = v_strip[:, :, kcols].transpose(0, 2, 1, 3, 4, 5).reshape(bsz, ncb, kr * NA_KBLOCK, NA_HEADS, NA_HEAD_DIM)
        q_blk = q_row.reshape(bsz, ncb, NA_QBLOCK, NA_HEADS, NA_HEAD_DIM)
        drow = rs + jnp.arange(kr) - r + NA_ROWS - 1
        bias = rpb[:, drow][:, :, dcol]
        bias = bias.transpose(0, 2, 3, 1, 4).reshape(NA_HEADS, ncb, NA_QBLOCK, kr * NA_KBLOCK)
        s_loc = jnp.einsum('bnqhd,bnkhd->bhnqk', q_blk, k_blk).astype(jnp.float32) + bias
        s_loc = jnp.where(mask, s_loc, NEG)
        s_meta = jnp.einsum('bnqhd,bmhd->bhnqm', q_blk, km).astype(jnp.float32)
        p = jax.nn.softmax(jnp.concatenate([s_meta, s_loc], axis=-1), axis=-1).astype(v.dtype)
        o = (jnp.einsum('bhnqm,bmhd->bnqhd', p[..., :N_META], vm)
             + jnp.einsum('bhnqk,bnkhd->bnqhd', p[..., N_META:], v_blk))
        return o.reshape(bsz, GRID_W, NA_HEADS, NA_HEAD_DIM)

    og = lax.map(row_fn, (jnp.arange(rows), qg.transpose(1, 0, 2, 3, 4)))
    og = og.transpose(1, 0, 2, 3, 4).reshape(bsz, n_tok, NA_WIDTH)
    return jnp.concatenate([om.reshape(bsz, N_META, NA_WIDTH), og], axis=1)


def chunk_recurrence(q, k, v, g):
    bsz, Lp, nh, dk = q.shape
    dv = v.shape[-1]
    nc = Lp // HG_CHUNK
    to_chunks = lambda t: t.reshape(bsz, nc, HG_CHUNK, nh, t.shape[-1]).transpose(1, 0, 3, 2, 4)
    tri = np.tril(np.ones((HG_CHUNK, HG_CHUNK), dtype=bool))[:, :, None]

    def step(S, xs):
        qc, kc, vc, gc = xs
        b = jnp.cumsum(gc, axis=2)
        o_inter = jnp.einsum('bhcd,bhde->bhce', qc * jnp.exp(b), S)
        diff = b[:, :, :, None, :] - b[:, :, None, :, :]
        decay = jnp.where(tri, jnp.exp(jnp.where(tri, diff, 0.0)), 0.0)
        att = jnp.einsum('bhid,bhjd,bhijd->bhij', qc, kc, decay)
        o_intra = jnp.einsum('bhij,bhje->bhie', att, vc)
        b_last = b[:, :, -1:, :]
        S = jnp.exp(b_last[:, :, 0, :, None]) * S + jnp.einsum('bhjd,bhje->bhde', kc * jnp.exp(b_last - b), vc)
        return S, o_inter + o_intra

    S0 = jnp.zeros((bsz, nh, dk, dv), jnp.float32)
    _, o = lax.scan(step, S0, (to_chunks(q), to_chunks(k), to_chunks(v), to_chunks(g)))
    return o.transpose(1, 0, 3, 2, 4).reshape(bsz, Lp, nh, dv)


def hgrn2_mixer(q, f_fwd, f_bwd, i, g_out, lb, onorm_g):
    f32 = jnp.float32
    bsz, L, _ = q.shape
    pad = HG_CHUNK - N_META
    heads = lambda t: jnp.pad(t.astype(f32), ((0, 0), (pad, 0), (0, 0))).reshape(bsz, L + pad, HG_HEADS, -1)
    qh = heads(jax.nn.silu(q.astype(f32)))
    vh = heads(i)
    lb = lb.astype(f32)

    def gates(f):
        ff = f.astype(f32)
        fg = lb + (1.0 - lb) * jax.nn.sigmoid(ff)
        log_f = jnp.log(fg)
        kk = (1.0 - lb) * jax.nn.sigmoid(-ff)
        return heads(kk), heads(log_f)

    kf, gf = gates(f_fwd)
    kb, gb = gates(f_bwd)
    flip = lambda t: jnp.flip(t, axis=1)
    o = chunk_recurrence(qh, kf, vh, gf) + flip(chunk_recurrence(flip(qh), flip(kb), flip(vh), flip(gb)))
    o = o[:, pad:]
    o = o * lax.rsqrt(jnp.mean(o * o, axis=-1, keepdims=True) + EPS) * onorm_g.astype(f32)
    o = o.reshape(bsz, L, HG_WIDTH) * jax.nn.silu(g_out.astype(f32))
    return o.astype(q.dtype)


def hybrid_layer(h, lb, norm1_g, w_in, s5_a_re, s5_a_im, s5_log_dt, s5_b_re, s5_b_im, s5_c_re, s5_c_im,
                 s5_d, s5_w_glu, na_rpb, hg_onorm_g, w_up_a, w_up_b, w_up_c, w_o, norm2_g,
                 w_ffn_gate, w_ffn_up, w_ffn_down):
    xn = rms_norm(h, norm1_g)
    z = xn @ w_in
    (u_a, q_b, k_b, v_b, q_c, f_cf, f_cb, i_c, g_c, gt_a, gt_b, gt_c) = jnp.split(
        z, list(np.cumsum(IN_SIZES)[:-1]), axis=-1)
    y_a = s5_mixer(u_a, s5_a_re, s5_a_im, s5_log_dt, s5_b_re, s5_b_im, s5_c_re, s5_c_im, s5_d, s5_w_glu) @ w_up_a
    y_b = neighborhood_attention(q_b, k_b, v_b, na_rpb) @ w_up_b
    y_c = hgrn2_mixer(q_c, f_cf, f_cb, i_c, g_c, lb, hg_onorm_g) @ w_up_c
    mix = jax.nn.sigmoid(gt_a) * y_a + jax.nn.sigmoid(gt_b) * y_b + jax.nn.sigmoid(gt_c) * y_c
    h = h + mix @ w_o
    hn = rms_norm(h, norm2_g)
    h = h + (jax.nn.silu(hn @ w_ffn_gate) * (hn @ w_ffn_up)) @ w_ffn_down
    return h


def trunk(x, lbs, meta_tokens, norm1_g, w_in, s5_a_re, s5_a_im, s5_log_dt, s5_b_re, s5_b_im, s5_c_re, s5_c_im,
          s5_d, s5_w_glu, na_rpb, hg_onorm_g, w_up_a, w_up_b, w_up_c, w_o, norm2_g,
          w_ffn_gate, w_ffn_up, w_ffn_down, final_norm_g):
    bsz = x.shape[0]
    meta = jnp.broadcast_to(meta_tokens[None].astype(x.dtype), (bsz, N_META, D_MODEL))
    h = jnp.concatenate([meta, x], axis=1)
    for l in range(DEPTH):
        h = hybrid_layer(h, lbs[l], norm1_g[l], w_in[l], s5_a_re[l], s5_a_im[l], s5_log_dt[l], s5_b_re[l],
                         s5_b_im[l], s5_c_re[l], s5_c_im[l], s5_d[l], s5_w_glu[l], na_rpb[l], hg_onorm_g[l],
                         w_up_a[l], w_up_b[l], w_up_c[l], w_o[l], norm2_g[l],
                         w_ffn_gate[l], w_ffn_up[l], w_ffn_down[l])
    h = rms_norm(h, final_norm_g)
    return h[:, N_META:]


def setup_inputs(seed: int = 0) -> dict:
    key = jax.random.key(seed)
    ks = jax.random.split(key, 32)
    nrm = lambda k, shape, s: jax.random.normal(k, shape, jnp.float32) * s
    G, P, C = S5_GROUPS, S5_STATE, S5_GROUP
    a_im_init = jnp.pi * jnp.arange(P, dtype=jnp.float32)
    return {
        "x_prompt": nrm(ks[0], (BATCH, SEQ, D_MODEL), 1.0),
        "x_sample": nrm(ks[1], (DEC_BATCH, DEC_SEQ, D_MODEL), 1.0),
        "meta_tokens": nrm(ks[2], (N_META, D_MODEL), 1.0),
        "norm1_g": 1.0 + nrm(ks[3], (DEPTH, D_MODEL), 0.1),
        "w_in": nrm(ks[4], (DEPTH, D_MODEL, IN_WIDTH), D_MODEL ** -0.5),
        "s5_a_re": -0.5 + nrm(ks[5], (DEPTH, 2, G, P), 0.01),
        "s5_a_im": a_im_init + nrm(ks[6], (DEPTH, 2, G, P), 0.01),
        "s5_log_dt": jax.random.uniform(ks[7], (DEPTH, 2, G), jnp.float32, math.log(1e-3), math.log(1e-1)),
        "s5_b_re": nrm(ks[8], (DEPTH, 2, G, P, C), (2 * C) ** -0.5),
        "s5_b_im": nrm(ks[9], (DEPTH, 2, G, P, C), (2 * C) ** -0.5),
        "s5_c_re": nrm(ks[10], (DEPTH, 2, G, C, P), P ** -0.5),
        "s5_c_im": nrm(ks[11], (DEPTH, 2, G, C, P), P ** -0.5),
        "s5_d": nrm(ks[12], (DEPTH, S5_WIDTH), 1.0),
        "s5_w_glu": nrm(ks[13], (DEPTH, S5_WIDTH, S5_WIDTH), S5_WIDTH ** -0.5),
        "na_rpb": nrm(ks[14], (DEPTH, NA_HEADS, 2 * NA_ROWS - 1, 2 * NA_COLS - 1), 0.1),
        "hg_lb_logits": nrm(ks[15], (DEPTH, HG_WIDTH), 1.0),
        "hg_onorm_g": 1.0 + nrm(ks[16], (DEPTH, HG_DV), 0.1),
        "w_up_a": nrm(ks[17], (DEPTH, S5_WIDTH, D_MODEL), S5_WIDTH ** -0.5),
        "w_up_b": nrm(ks[18], (DEPTH, NA_WIDTH, D_MODEL), NA_WIDTH ** -0.5),
        "w_up_c": nrm(ks[19], (DEPTH, HG_WIDTH, D_MODEL), HG_WIDTH ** -0.5),
        "w_o": nrm(ks[20], (DEPTH, D_MODEL, D_MODEL), D_MODEL ** -0.5),
        "norm2_g": 1.0 + nrm(ks[21], (DEPTH, D_MODEL), 0.1),
        "w_ffn_gate": nrm(ks[22], (DEPTH, D_MODEL, FFN_HIDDEN), D_MODEL ** -0.5),
        "w_ffn_up": nrm(ks[23], (DEPTH, D_MODEL, FFN_HIDDEN), D_MODEL ** -0.5),
        "w_ffn_down": nrm(ks[24], (DEPTH, FFN_HIDDEN, D_MODEL), FFN_HIDDEN ** -0.5),
        "final_norm_g": 1.0 + nrm(ks[25], (D_MODEL,), 0.1),
    }


def reference(x_prompt, x_sample, meta_tokens, norm1_g, w_in, s5_a_re, s5_a_im, s5_log_dt, s5_b_re, s5_b_im,
              s5_c_re, s5_c_im, s5_d, s5_w_glu, na_rpb, hg_lb_logits, hg_onorm_g, w_up_a, w_up_b, w_up_c,
              w_o, norm2_g, w_ffn_gate, w_ffn_up, w_ffn_down, final_norm_g):
    sm = jax.nn.softmax(hg_lb_logits.astype(jnp.float32), axis=0)
    lbs = jnp.cumsum(sm, axis=0) - sm[0:1]
    y_prompt = trunk(x_prompt, lbs, meta_tokens, norm1_g, w_in, s5_a_re, s5_a_im, s5_log_dt, s5_b_re, s5_b_im,
                     s5_c_re, s5_c_im, s5_d, s5_w_glu, na_rpb, hg_onorm_g, w_up_a, w_up_b, w_up_c, w_o,
                     norm2_g, w_ffn_gate, w_ffn_up, w_ffn_down, final_norm_g)
    y_sample = trunk(x_sample, lbs, meta_tokens, norm1_g, w_in, s5_a_re, s5_a_im, s5_log_dt, s5_b_re, s5_b_im,
                     s5_c_re, s5_c_im, s5_d, s5_w_glu, na_rpb, hg_onorm_g, w_up_a, w_up_b, w_up_c, w_o,
                     norm2_g, w_ffn_gate, w_ffn_up, w_ffn_down, final_norm_g)
    return (y_prompt, y_sample)
```

```python
import functools

import numpy as np
import jax
import jax.numpy as jnp
from jax import lax
from jax.experimental import pallas as pl
from jax.experimental.pallas import tpu as pltpu

F32 = jnp.float32
MM_DTYPE = jnp.bfloat16

D_MODEL = 1024
N_META = 16
GRID_W = 64
EPS = 1e-6
NEG = -1e30
BLK = 64
PAD = BLK - N_META
S5_WIDTH = 256
S5_GROUP = 16
S5_GROUPS = 16
S5_STATE = 64
S5_T = 64
NA_HEADS = 8
NA_HEAD_DIM = 64
NA_WIDTH = 512
NA_ROWS = 8
NA_COLS = 16
NA_HG = 4
NA_QROWS = 8
HG_HEADS = 4
HG_DK = 64
HG_WIDTH = 256
HG_SUB = 16
FFN_HIDDEN = 2816
FFN_CHUNK = 256
IN_WIDTH = 6144
QKV_W = 3 * NA_WIDTH
HG_IN_W = 5 * HG_WIDTH
GATE_W = 3 * D_MODEL
ROW_TILE_IN = 256
ROW_TILE_OUT = 256
VMEM_LIMIT = 56 * 1024 * 1024


def _dot(a, b):
    return jnp.dot(a, b, preferred_element_type=F32)


def _dot_nt(a, b):
    return lax.dot_general(a, b, (((1,), (1,)), ((), ())), preferred_element_type=F32)


def _dot_tn(a, b):
    return lax.dot_general(a, b, (((0,), (0,)), ((), ())), preferred_element_type=F32)


def _dot_f32(a, b):
    return jnp.dot(a, b, preferred_element_type=F32, precision=lax.Precision.HIGHEST)


def _mm(x):
    return x.astype(MM_DTYPE)


def _const_spec(shape):
    nd = len(shape)
    return pl.BlockSpec(shape, lambda *_: (0,) * nd, pipeline_mode=pl.Buffered(1))


def _params(sem):
    return pltpu.CompilerParams(dimension_semantics=sem, vmem_limit_bytes=VMEM_LIMIT)


def _in_proj_body(h_ref, g_ref, w_ref, ua_ref, qkv_ref, hg_ref, gt_ref):
    x = h_ref[...]
    xn = x * lax.rsqrt(jnp.mean(x * x, axis=-1, keepdims=True) + EPS) * g_ref[...]
    xn = _mm(xn)
    col = 0
    for ref, width in ((ua_ref, S5_WIDTH), (qkv_ref, QKV_W), (hg_ref, HG_IN_W), (gt_ref, GATE_W)):
        for c0 in range(0, width, 256):
            ref[:, c0:c0 + 256] = _dot(xn, w_ref[:, col + c0:col + c0 + 256]).astype(ref.dtype)
        col += width


def _in_proj(h2d, g, w):
    rows = h2d.shape[0]
    tm = ROW_TILE_IN
    row = lambda width: pl.BlockSpec((tm, width), lambda i: (i, 0))
    return pl.pallas_call(
        _in_proj_body,
        grid=(pl.cdiv(rows, tm),),
        in_specs=[row(D_MODEL), _const_spec((1, D_MODEL)), _const_spec((D_MODEL, IN_WIDTH))],
        out_specs=[row(S5_WIDTH), row(QKV_W), row(HG_IN_W), row(GATE_W)],
        out_shape=[
            jax.ShapeDtypeStruct((rows, S5_WIDTH), F32),
            jax.ShapeDtypeStruct((rows, QKV_W), MM_DTYPE),
            jax.ShapeDtypeStruct((rows, HG_IN_W), F32),
            jax.ShapeDtypeStruct((rows, GATE_W), F32),
        ],
        compiler_params=_params(("parallel",)),
        name="in_proj",
    )(h2d, g, w)


def _s5_operators(a_re, a_im, log_dt, b_re, b_im, c_re, c_im):
    T, G, P, C = S5_T, S5_GROUPS, S5_STATE, S5_GROUP
    hi = lax.Precision.HIGHEST
    dt = jnp.exp(log_dt)[..., None]
    mag = jnp.exp(a_re * dt)
    lr = mag * jnp.cos(a_im * dt)
    li = mag * jnp.sin(a_im * dt)
    den = a_re * a_re + a_im * a_im
    nr = lr - 1.0
    z_re = (nr * a_re + li * a_im) / den
    z_im = (li * a_re - nr * a_im) / den
    bb_re = z_re[..., None] * b_re - z_im[..., None] * b_im
    bb_im = z_re[..., None] * b_im + z_im[..., None] * b_re
    d = jnp.arange(T + 1, dtype=F32)
    pmag = jnp.exp((a_re * dt)[..., None] * d)
    pr = pmag * jnp.cos((a_im * dt)[..., None] * d)
    pi = pmag * jnp.sin((a_im * dt)[..., None] * d)
    lb_re = pr[..., None] * bb_re[:, :, :, None, :] - pi[..., None] * bb_im[:, :, :, None, :]
    lb_im = pr[..., None] * bb_im[:, :, :, None, :] + pi[..., None] * bb_re[:, :, :, None, :]
    cl_re = c_re[..., None] * pr[:, :, None] - c_im[..., None] * pi[:, :, None]
    cl_im = c_re[..., None] * pi[:, :, None] + c_im[..., None] * pr[:, :, None]
    kk = (jnp.einsum('xgop,xgpdi->xgdoi', c_re, lb_re, precision=hi)
          - jnp.einsum('xgop,xgpdi->xgdoi', c_im, lb_im, precision=hi))[:, :, :T]
    kf, kb = kk[0], kk[1]
    kall = jnp.concatenate([kb[:, :0:-1], (kf[:, :1] + kb[:, :1]), kf[:, 1:]], axis=1)
    idx = np.arange(T)[None, :] - np.arange(T)[:, None] + T - 1
    m = kall[:, idx]
    m = m.transpose(0, 1, 4, 2, 3).reshape(G, T * C, T * C)

    def pack_cols(re, im):
        z = jnp.zeros_like(re)
        odd = (jnp.arange(G) % 2 == 1)[:, None, None]
        return jnp.concatenate([jnp.where(odd, z, re), jnp.where(odd, re, z),
                                jnp.where(odd, z, im), jnp.where(odd, im, z)], axis=-1)

    vf_re = lb_re[0][:, :, T - 1::-1][:, :, :T].transpose(0, 2, 3, 1).reshape(G, T * C, P)
    vf_im = lb_im[0][:, :, T - 1::-1][:, :, :T].transpose(0, 2, 3, 1).reshape(G, T * C, P)
    vb_re = lb_re[1][:, :, :T].transpose(0, 2, 3, 1).reshape(G, T * C, P)
    vb_im = lb_im[1][:, :, :T].transpose(0, 2, 3, 1).reshape(G, T * C, P)
    v_f = pack_cols(vf_re, vf_im)
    v_b = pack_cols(vb_re, vb_im)
    wf_re = cl_re[0][..., 1:T + 1].transpose(0, 2, 3, 1).reshape(G, P, T * C)
    wf_im = cl_im[0][..., 1:T + 1].transpose(0, 2, 3, 1).reshape(G, P, T * C)
    wb_re = cl_re[1][..., T:0:-1].transpose(0, 2, 3, 1).reshape(G, P, T * C)
    wb_im = cl_im[1][..., T:0:-1].transpose(0, 2, 3, 1).reshape(G, P, T * C)
    pack_rows = lambda re, im: pack_cols(re.transpose(0, 2, 1), -im.transpose(0, 2, 1)).transpose(0, 2, 1)
    w_f = pack_rows(wf_re, wf_im)
    w_b = pack_rows(wb_re, wb_im)
    lam = jnp.stack([pr[0, :, :, T], pi[0, :, :, T], pr[1, :, :, T], pi[1, :, :, T]], axis=0)
    lam = lam.reshape(4, G // 2, 2 * P).transpose(1, 0, 2)
    return _mm(m), _mm(v_f), _mm(v_b), _mm(w_f), _mm(w_b), lam


def _s5_inc_body(u_ref, vf_ref, vb_ref, incf_ref, incb_ref):
    incf_ref[...] = _dot(u_ref[0], vf_ref[0]) + _dot(u_ref[1], vf_ref[1])
    incb_ref[...] = _dot(u_ref[0], vb_ref[0]) + _dot(u_ref[1], vb_ref[1])


def _s5_scan_body(incf_ref, incb_ref, lam_ref, xf_ref, xb_ref):
    nc, bsz, _ = incf_ref.shape
    hw = S5_STATE * 2
    lam = lam_ref[0]
    zero = jnp.zeros((bsz, hw), F32)

    def run(inc_ref, x_ref, lr, li, reverse):
        def step(i, carry):
            n = nc - 1 - i if reverse else i
            xr, xi = carry
            x_ref[n, :, 0:hw] = xr
            x_ref[n, :, hw:2 * hw] = xi
            inc = inc_ref[n]
            return (lr * xr - li * xi + inc[:, 0:hw], lr * xi + li * xr + inc[:, hw:2 * hw])

        lax.fori_loop(0, nc, step, (zero, zero))

    run(incf_ref, xf_ref, lam[0:1], lam[1:2], False)
    run(incb_ref, xb_ref, lam[2:3], lam[3:4], True)


def _s5_out_body(u_ref, m_ref, wf_ref, wb_ref, xf_ref, xb_ref, y_ref):
    xf = _mm(xf_ref[...])
    xb = _mm(xb_ref[...])
    for k in range(2):
        y_ref[k] = _dot(u_ref[k], m_ref[k]) + _dot(xf, wf_ref[k]) + _dot(xb, wb_ref[k])


def _s5_scan(u_a, ops, bsz, n_tok):
    m, v_f, v_b, w_f, w_b, lam = ops
    T, G, C = S5_T, S5_GROUPS, S5_GROUP
    l_pad = n_tok + BLK
    nc = l_pad // T
    rows = nc * bsz
    tw = T * C
    sw = 4 * S5_STATE
    seq = jnp.concatenate([u_a[:, n_tok:], u_a[:, :n_tok]], axis=1)
    u = _mm(seq).reshape(bsz, nc, T, G, C).transpose(3, 1, 0, 2, 4).reshape(G, rows, tw)
    pair3 = lambda r, c: pl.BlockSpec((2, r, c), lambda p: (p, 0, 0))
    lane = lambda r: pl.BlockSpec((r, sw), lambda p: (0, p))
    inc_f, inc_b = pl.pallas_call(
        _s5_inc_body,
        grid=(G // 2,),
        in_specs=[pair3(rows, tw), pair3(tw, sw), pair3(tw, sw)],
        out_specs=[lane(rows), lane(rows)],
        out_shape=[jax.ShapeDtypeStruct((rows, sw * G // 2), F32)] * 2,
        compiler_params=_params(("parallel",)),
        name="s5_inc",
    )(u, v_f, v_b)
    lane3 = pl.BlockSpec((nc, bsz, sw), lambda p: (0, 0, p))
    x_f, x_b = pl.pallas_call(
        _s5_scan_body,
        grid=(G // 2,),
        in_specs=[lane3, lane3, pl.BlockSpec((1, 4, 2 * S5_STATE), lambda p: (p, 0, 0))],
        out_specs=[lane3, lane3],
        out_shape=[jax.ShapeDtypeStruct((nc, bsz, sw * G // 2), F32)] * 2,
        compiler_params=_params(("parallel",)),
        name="s5_chunk_scan",
    )(inc_f.reshape(nc, bsz, -1), inc_b.reshape(nc, bsz, -1), lam)
    y = pl.pallas_call(
        _s5_out_body,
        grid=(G // 2,),
        in_specs=[pair3(rows, tw), pair3(tw, tw), pair3(sw, tw), pair3(sw, tw), lane(rows), lane(rows)],
        out_specs=pair3(rows, tw),
        out_shape=jax.ShapeDtypeStruct((G, rows, tw), F32),
        compiler_params=_params(("parallel",)),
        name="s5_out",
    )(u, m, w_f, w_b, x_f.reshape(rows, -1), x_b.reshape(rows, -1))
    y = y.reshape(G, nc, bsz, T, C).transpose(2, 1, 3, 0, 4).reshape(bsz, l_pad, S5_WIDTH)
    return jnp.concatenate([y[:, BLK:], y[:, :BLK]], axis=1)


def _na_bias(rpb):
    cols = np.arange(GRID_W)
    wstart = np.clip(cols - NA_COLS // 2, 0, GRID_W - NA_COLS)
    inwin = (cols[None, :] >= wstart[:, None]) & (cols[None, :] < wstart[:, None] + NA_COLS)
    dcol = np.clip(cols[None, :] - cols[:, None], -(NA_COLS - 1), NA_COLS - 1) + NA_COLS - 1
    drow = np.arange(NA_ROWS)[None, :] - np.arange(NA_ROWS)[:, None] + NA_ROWS - 1
    b = rpb.astype(F32)[:, drow][:, :, :, dcol]
    b = jnp.where(inwin[None, None, None], b, NEG)
    b = b.transpose(1, 0, 3, 2, 4)
    return b.reshape(NA_ROWS, NA_HEADS // NA_HG, NA_HG * GRID_W, NA_ROWS * GRID_W)


def _head_select(o, rows_per_head, lane_head):
    out = jnp.where(lane_head == 0, o[0:rows_per_head], 0.0)
    for h in range(1, NA_HG):
        out = out + jnp.where(lane_head == h, o[h * rows_per_head:(h + 1) * rows_per_head], 0.0)
    return out


def _na_body(q_ref, k_ref, v_ref, bias_ref, o_ref, *, n_tok):
    rg = pl.program_id(2)
    n_rg = n_tok // (NA_QROWS * GRID_W)
    grid_rows = n_tok // GRID_W
    meta0 = n_tok + PAD
    scale = NA_HEAD_DIM ** -0.5
    hw = NA_HG * NA_HEAD_DIM
    km = k_ref[0, meta0:meta0 + N_META, :]
    vm = v_ref[0, meta0:meta0 + N_META, :]

    def masked_queries(q, nq):
        row_head = lax.broadcasted_iota(jnp.int32, (NA_HG * nq, hw), 0) // nq
        lane_head = lax.broadcasted_iota(jnp.int32, (NA_HG * nq, hw), 1) // NA_HEAD_DIM
        qq = jnp.concatenate([q] * NA_HG, axis=0)
        return jnp.where(row_head == lane_head, qq, jnp.zeros_like(qq))

    @pl.when(rg < n_rg)
    def _grid_rows():
        lane_head = lax.broadcasted_iota(jnp.int32, (GRID_W, hw), 1) // NA_HEAD_DIM

        def row_step(rr, carry):
            r = rg * NA_QROWS + rr
            rs = jnp.clip(r - NA_ROWS // 2, 0, grid_rows - NA_ROWS)
            q0 = pl.multiple_of(rr * GRID_W, GRID_W)
            k0 = pl.multiple_of(rs * GRID_W, GRID_W)
            q = q_ref[0, pl.ds(q0, GRID_W), :] * scale
            qm = masked_queries(q.astype(q_ref.dtype), GRID_W)
            ks = k_ref[0, pl.ds(k0, NA_ROWS * GRID_W), :]
            vs = v_ref[0, pl.ds(k0, NA_ROWS * GRID_W), :]
            s = _dot_nt(qm, ks) + bias_ref[r - rs, 0]
            sm = _dot_nt(qm, km)
            mx = jnp.maximum(jnp.max(s, axis=-1, keepdims=True), jnp.max(sm, axis=-1, keepdims=True))
            p = jnp.exp(s - mx)
            pm = jnp.exp(sm - mx)
            den = jnp.sum(p, axis=-1, keepdims=True) + jnp.sum(pm, axis=-1, keepdims=True)
            o = (_dot(_mm(p), vs) + _dot(_mm(pm), vm)) / den
            o_ref[0, pl.ds(q0, GRID_W), :] = _head_select(o, GRID_W, lane_head).astype(o_ref.dtype)
            return carry

        lax.fori_loop(0, NA_QROWS, row_step, 0)

    @pl.when(rg == n_rg)
    def _meta_rows():
        lane_head = lax.broadcasted_iota(jnp.int32, (N_META, hw), 1) // NA_HEAD_DIM
        q = q_ref[0, PAD:PAD + N_META, :] * scale
        qm = masked_queries(q.astype(q_ref.dtype), N_META)
        s = _dot_nt(qm, km)
        p = jnp.exp(s - jnp.max(s, axis=-1, keepdims=True))
        o = _dot(_mm(p), vm) / jnp.sum(p, axis=-1, keepdims=True)
        o_ref[0, 0:PAD, :] = jnp.zeros((PAD, hw), o_ref.dtype)
        o_ref[0, PAD:BLK, :] = _head_select(o, N_META, lane_head).astype(o_ref.dtype)


def _neighborhood_attention(qkv, bias, bsz, n_tok):
    l_pad = n_tok + BLK
    hw = NA_HG * NA_HEAD_DIM
    n_hg = NA_HEADS // NA_HG
    qrows = NA_QROWS * GRID_W
    whole = lambda off: pl.BlockSpec((1, l_pad, hw), lambda b, g, r: (b, 0, off + g), pipeline_mode=pl.Buffered(1))
    return pl.pallas_call(
        functools.partial(_na_body, n_tok=n_tok),
        grid=(bsz, n_hg, n_tok // qrows + 1),
        in_specs=[
            pl.BlockSpec((1, qrows, hw), lambda b, g, r: (b, r, g)),
            whole(n_hg),
            whole(2 * n_hg),
            pl.BlockSpec((NA_ROWS, 1, NA_HG * GRID_W, NA_ROWS * GRID_W), lambda b, g, r: (0, g, 0, 0)),
        ],
        out_specs=pl.BlockSpec((1, qrows, hw), lambda b, g, r: (b, r, g)),
        out_shape=jax.ShapeDtypeStruct((bsz, l_pad, NA_WIDTH), MM_DTYPE),
        compiler_params=_params(("parallel", "parallel", "arbitrary")),
        name="neighborhood_attention",
    )(qkv, qkv, qkv, bias)


def _hgrn_chunk(qraw, ff, v, lb, st_ref, rev):
    C, c = BLK, HG_SUB
    nsub = C // c
    w = HG_WIDTH
    g = jnp.log(lb + (1.0 - lb) * jax.nn.sigmoid(ff))
    k = (1.0 - lb) * jax.nn.sigmoid(-ff)
    q = qraw * jax.nn.sigmoid(qraw)
    ri = lax.broadcasted_iota(jnp.int32, (C, C), 0)
    ci = lax.broadcasted_iota(jnp.int32, (C, C), 1)
    tri = jnp.where((ci >= ri) if rev else (ci <= ri), 1.0, 0.0).astype(F32)
    b = _dot_f32(tri, g)
    edge = b[0:1] if rev else b[C - 1:C]
    st = st_ref[...]
    o_inter = _dot_nt(_mm(q * jnp.exp(b)), _mm(st))
    ke = k * jnp.exp(edge - b)
    d_t = _dot_tn(_mm(v), _mm(ke))
    blk_r = lax.broadcasted_iota(jnp.int32, (w, w), 0) // HG_DK
    blk_c = lax.broadcasted_iota(jnp.int32, (w, w), 1) // HG_DK
    same_head = blk_r == blk_c
    st_ref[...] = st * jnp.exp(edge) + jnp.where(same_head, d_t, 0.0)
    hsum = jnp.where(same_head, 1.0, 0.0).astype(MM_DTYPE)
    row_head = lax.broadcasted_iota(jnp.int32, (HG_HEADS * c, w), 0) // c
    lane_head_q = lax.broadcasted_iota(jnp.int32, (HG_HEADS * c, w), 1) // HG_DK
    lane_head = lax.broadcasted_iota(jnp.int32, (c, w), 1) // HG_DK
    jj = lax.broadcasted_iota(jnp.int32, (c, c, w), 0)
    ii = lax.broadcasted_iota(jnp.int32, (c, c, w), 1)
    causal = (jj >= ii) if rev else (jj <= ii)
    outs = []
    for s in range(nsub):
        lo, hi = c * s, c * (s + 1)
        if rev:
            klo, khi = hi, C
            beta = b[hi:hi + 1] if s < nsub - 1 else None
        else:
            klo, khi = 0, lo
            beta = b[lo - 1:lo] if s > 0 else None
        bs, qs, ks, vs = b[lo:hi], q[lo:hi], k[lo:hi], v[lo:hi]
        acc = o_inter[lo:hi]
        if khi > klo:
            qt = qs * jnp.exp(bs - beta)
            kt = k[klo:khi] * jnp.exp(beta - b[klo:khi])
            qh = jnp.where(row_head == lane_head_q, jnp.concatenate([qt] * HG_HEADS, axis=0), 0.0)
            att = _dot_nt(_mm(qh), _mm(kt))
            r = _dot(_mm(att), _mm(v[klo:khi]))
            for h in range(HG_HEADS):
                acc = acc + jnp.where(lane_head == h, r[h * c:(h + 1) * c], 0.0)
        diff = bs[None, :, :] - bs[:, None, :]
        dec = jnp.where(causal, jnp.exp(jnp.where(causal, diff, 0.0)), 0.0)
        p = (ks[:, None, :] * qs[None, :, :] * dec).reshape(c * c, w)
        a = _dot(_mm(p), hsum).reshape(c, c, w)
        acc = acc + jnp.sum(a * vs[:, None, :], axis=0)
        outs.append(acc)
    return jnp.concatenate(outs, axis=0)


def _hgrn_body(qf_ref, ff_ref, vf_ref, qb_ref, fb_ref, vb_ref, lb_ref, of_ref, ob_ref, stf_ref, stb_ref):
    @pl.when(pl.program_id(1) == 0)
    def _reset():
        stf_ref[...] = jnp.zeros_like(stf_ref)
        stb_ref[...] = jnp.zeros_like(stb_ref)

    lb = lb_ref[...]
    of_ref[0] = _hgrn_chunk(qf_ref[0], ff_ref[0], vf_ref[0], lb, stf_ref, False)
    ob_ref[0] = _hgrn_chunk(qb_ref[0], fb_ref[0], vb_ref[0], lb, stb_ref, True)


def _hgrn_scan(hg, lb, bsz, n_tok):
    l_pad = n_tok + BLK
    nb = l_pad // BLK
    m = nb - 1
    fwd = lambda n: (n + m) % nb
    bwd = lambda n: jnp.where(n == m, m, m - 1 - n)
    spec = lambda order, col: pl.BlockSpec((1, BLK, HG_WIDTH), lambda b, n: (b, order(n), col))
    return pl.pallas_call(
        _hgrn_body,
        grid=(bsz, nb),
        in_specs=[spec(fwd, 0), spec(fwd, 1), spec(fwd, 3), spec(bwd, 0), spec(bwd, 2), spec(bwd, 3),
                  pl.BlockSpec((1, HG_WIDTH), lambda b, n: (0, 0))],
        out_specs=[spec(fwd, 0), spec(bwd, 0)],
        out_shape=[jax.ShapeDtypeStruct((bsz, l_pad, HG_WIDTH), F32)] * 2,
        scratch_shapes=[pltpu.VMEM((HG_WIDTH, HG_WIDTH), F32)] * 2,
        compiler_params=_params(("parallel", "arbitrary")),
        name="hgrn2_scan",
    )(hg, hg, hg, hg, hg, hg, lb)


def _out_ffn_body(h_ref, ys_ref, ua_ref, att_ref, of_ref, ob_ref, gc_ref, gt_ref,
                  d_ref, wglu_ref, on_ref, wa_ref, wb_ref, wc_ref, wo_ref, n2_ref, wg_ref, wu_ref, wd_ref,
                  out_ref, *, blocks_per_seq):
    tm = h_ref.shape[0]
    y = jax.nn.gelu(ys_ref[...] + d_ref[...] * ua_ref[...])
    y = y * jax.nn.sigmoid(_dot(_mm(y), wglu_ref[...]))
    o = of_ref[...] + ob_ref[...]
    blk_r = lax.broadcasted_iota(jnp.int32, (HG_WIDTH, HG_WIDTH), 0) // HG_DK
    blk_c = lax.broadcasted_iota(jnp.int32, (HG_WIDTH, HG_WIDTH), 1) // HG_DK
    head_mean = jnp.where(blk_r == blk_c, 1.0 / HG_DK, 0.0).astype(F32)
    o = o * lax.rsqrt(_dot_f32(o * o, head_mean) + EPS) * on_ref[...]
    gc = gc_ref[...]
    o = o * (gc * jax.nn.sigmoid(gc))
    mix = (jax.nn.sigmoid(gt_ref[:, 0:D_MODEL]) * _dot(_mm(y), wa_ref[...])
           + jax.nn.sigmoid(gt_ref[:, D_MODEL:2 * D_MODEL]) * _dot(att_ref[...], wb_ref[...])
           + jax.nn.sigmoid(gt_ref[:, 2 * D_MODEL:3 * D_MODEL]) * _dot(_mm(o), wc_ref[...]))
    h = h_ref[...] + _dot(_mm(mix), wo_ref[...])
    hn = _mm(h * lax.rsqrt(jnp.mean(h * h, axis=-1, keepdims=True) + EPS) * n2_ref[...])
    for c0 in range(0, FFN_HIDDEN, FFN_CHUNK):
        gate = _dot(hn, wg_ref[:, c0:c0 + FFN_CHUNK])
        up = _dot(hn, wu_ref[:, c0:c0 + FFN_CHUNK])
        h = h + _dot(_mm(gate * jax.nn.sigmoid(gate) * up), wd_ref[c0:c0 + FFN_CHUNK, :])
    out_ref[...] = h
    first_blk = pl.program_id(0) * (tm // BLK)
    for j in range(tm // BLK):
        @pl.when((first_blk + j) % blocks_per_seq == blocks_per_seq - 1)
        def _zero_rows():
            out_ref[j * BLK:j * BLK + PAD, :] = jnp.zeros((PAD, D_MODEL), F32)


def _out_ffn(h2d, ys, ua, att, o_f, o_b, hg, gates, lw, blocks_per_seq):
    rows = h2d.shape[0]
    tm = ROW_TILE_OUT
    row = lambda width, col=0: pl.BlockSpec((tm, width), lambda i: (i, col))
    weights = [lw["d"], lw["w_glu"], lw["onorm"], lw["w_up_a"], lw["w_up_b"], lw["w_up_c"], lw["w_o"],
               lw["norm2"], lw["w_ffn_gate"], lw["w_ffn_up"], lw["w_ffn_down"]]
    return pl.pallas_call(
        functools.partial(_out_ffn_body, blocks_per_seq=blocks_per_seq),
        grid=(pl.cdiv(rows, tm),),
        in_specs=[row(D_MODEL), row(S5_WIDTH), row(S5_WIDTH), row(NA_WIDTH), row(HG_WIDTH), row(HG_WIDTH),
                  row(HG_WIDTH, 4), row(GATE_W)] + [_const_spec(w.shape) for w in weights],
        out_specs=row(D_MODEL),
        out_shape=jax.ShapeDtypeStruct((rows, D_MODEL), F32),
        compiler_params=_params(("parallel",)),
        name="mix_out_ffn",
    )(h2d, ys, ua, att, o_f, o_b, hg, gates, *weights)


def _final_norm_body(h_ref, g_ref, o_ref):
    x = h_ref[0]
    o_ref[0] = x * lax.rsqrt(jnp.mean(x * x, axis=-1, keepdims=True) + EPS) * g_ref[...]


def _final_norm(h, g, n_tok):
    bsz = h.shape[0]
    tl = 512
    spec = pl.BlockSpec((1, tl, D_MODEL), lambda b, j: (b, j, 0))
    return pl.pallas_call(
        _final_norm_body,
        grid=(bsz, n_tok // tl),
        in_specs=[spec, pl.BlockSpec((1, D_MODEL), lambda b, j: (0, 0))],
        out_specs=spec,
        out_shape=jax.ShapeDtypeStruct((bsz, n_tok, D_MODEL), F32),
        compiler_params=_params(("parallel", "parallel")),
        name="final_norm",
    )(h, g)


def _trunk(x, meta_tokens, layers, final_g):
    bsz, n_tok, _ = x.shape
    l_pad = n_tok + BLK
    rows = bsz * l_pad
    meta = jnp.broadcast_to(meta_tokens[None].astype(F32), (bsz, N_META, D_MODEL))
    h = jnp.concatenate([x.astype(F32), jnp.zeros((bsz, PAD, D_MODEL), F32), meta], axis=1).reshape(rows, D_MODEL)
    for lw in layers:
        ua, qkv, hg, gates = _in_proj(h, lw["norm1"], lw["w_in"])
        ys = _s5_scan(ua.reshape(bsz, l_pad, S5_WIDTH), lw["s5_ops"], bsz, n_tok).reshape(rows, S5_WIDTH)
        att = _neighborhood_attention(qkv.reshape(bsz, l_pad, QKV_W), lw["na_bias"], bsz, n_tok)
        o_f, o_b = _hgrn_scan(hg.reshape(bsz, l_pad, HG_IN_W), lw["lb"], bsz, n_tok)
        h = _out_ffn(h, ys, ua, att.reshape(rows, NA_WIDTH), o_f.reshape(rows, HG_WIDTH),
                     o_b.reshape(rows, HG_WIDTH), hg, gates, lw, l_pad // BLK)
    return _final_norm(h.reshape(bsz, l_pad, D_MODEL), final_g, n_tok)


def kernel(x_prompt, x_sample, meta_tokens, norm1_g, w_in, s5_a_re, s5_a_im, s5_log_dt, s5_b_re, s5_b_im,
           s5_c_re, s5_c_im, s5_d, s5_w_glu, na_rpb, hg_lb_logits, hg_onorm_g, w_up_a, w_up_b, w_up_c,
           w_o, norm2_g, w_ffn_gate, w_ffn_up, w_ffn_down, final_norm_g):
    depth = w_in.shape[0]
    sm = jax.nn.softmax(hg_lb_logits.astype(F32), axis=0)
    lbs = jnp.cumsum(sm, axis=0) - sm[0:1]
    layers = []
    for l in range(depth):
        layers.append(dict(
            norm1=norm1_g[l].astype(F32)[None],
            w_in=_mm(w_in[l]),
            s5_ops=_s5_operators(s5_a_re[l].astype(F32), s5_a_im[l].astype(F32), s5_log_dt[l].astype(F32),
                                 s5_b_re[l].astype(F32), s5_b_im[l].astype(F32), s5_c_re[l].astype(F32),
                                 s5_c_im[l].astype(F32)),
            na_bias=_na_bias(na_rpb[l]),
            lb=lbs[l][None],
            d=s5_d[l].astype(F32)[None],
            w_glu=_mm(s5_w_glu[l]),
            onorm=jnp.tile(hg_onorm_g[l].astype(F32), HG_HEADS)[None],
            w_up_a=_mm(w_up_a[l]), w_up_b=_mm(w_up_b[l]), w_up_c=_mm(w_up_c[l]), w_o=_mm(w_o[l]),
            norm2=norm2_g[l].astype(F32)[None],
            w_ffn_gate=_mm(w_ffn_gate[l]), w_ffn_up=_mm(w_ffn_up[l]), w_ffn_down=_mm(w_ffn_down[l]),
        ))
    final_g = final_norm_g.astype(F32)[None]
    return (_trunk(x_prompt, meta_tokens, layers, final_g), _trunk(x_sample, meta_tokens, layers, final_g))
```

```python
import functools

import numpy as np
import jax
import jax.numpy as jnp
from jax import lax
from jax.experimental import pallas as pl
from jax.experimental.pallas import tpu as pltpu

F32 = jnp.float32
MM_DTYPE = jnp.bfloat16

D_MODEL = 1024
N_META = 16
GRID_W = 64
EPS = 1e-6
NEG = -1e30
BLK = 64
PAD = BLK - N_META
S5_WIDTH = 256
S5_GROUP = 16
S5_GROUPS = 16
S5_STATE = 64
S5_T = 64
S5_PACK_CHUNKS = 16
NA_HEADS = 8
NA_HEAD_DIM = 64
NA_WIDTH = 512
NA_ROWS = 8
NA_COLS = 16
NA_HG = 4
NA_QROWS = 8
HG_HEADS = 4
HG_DK = 64
HG_WIDTH = 256
HG_SUB = 8
FFN_HIDDEN = 2816
FFN_CHUNK = 256
IN_WIDTH = 6144
QKV_W = 3 * NA_WIDTH
HG_IN_W = 5 * HG_WIDTH
GATE_W = 3 * D_MODEL
ROW_TILE_IN = 256
ROW_TILE_OUT = 256
VMEM_LIMIT = 56 * 1024 * 1024


def _dot(a, b):
    return jnp.dot(a, b, preferred_element_type=F32)


def _dot_nt(a, b):
    return lax.dot_general(a, b, (((1,), (1,)), ((), ())), preferred_element_type=F32)


def _dot_tn(a, b):
    return lax.dot_general(a, b, (((0,), (0,)), ((), ())), preferred_element_type=F32)


def _dot_f32(a, b):
    return jnp.dot(a, b, preferred_element_type=F32, precision=lax.Precision.HIGHEST)


def _mm(x):
    return x.astype(MM_DTYPE)


def _const_spec(shape):
    nd = len(shape)
    return pl.BlockSpec(shape, lambda *_: (0,) * nd, pipeline_mode=pl.Buffered(1))


def _params(sem):
    return pltpu.CompilerParams(dimension_semantics=sem, vmem_limit_bytes=VMEM_LIMIT)


def _in_proj_body(h_ref, g_ref, w_ref, ua_ref, qkv_ref, hg_ref, gt_ref):
    x = h_ref[...]
    xn = x * lax.rsqrt(jnp.mean(x * x, axis=-1, keepdims=True) + EPS) * g_ref[...]
    xn = _mm(xn)
    col = 0
    for ref, width in ((ua_ref, S5_WIDTH), (qkv_ref, QKV_W), (hg_ref, HG_IN_W), (gt_ref, GATE_W)):
        for c0 in range(0, width, 256):
            ref[:, c0:c0 + 256] = _dot(xn, w_ref[:, col + c0:col + c0 + 256]).astype(ref.dtype)
        col += width


def _in_proj(h2d, g, w):
    rows = h2d.shape[0]
    tm = ROW_TILE_IN
    row = lambda width: pl.BlockSpec((tm, width), lambda i: (i, 0))
    return pl.pallas_call(
        _in_proj_body,
        grid=(pl.cdiv(rows, tm),),
        in_specs=[row(D_MODEL), _const_spec((1, D_MODEL)), _const_spec((D_MODEL, IN_WIDTH))],
        out_specs=[row(S5_WIDTH), row(QKV_W), row(HG_IN_W), row(GATE_W)],
        out_shape=[
            jax.ShapeDtypeStruct((rows, S5_WIDTH), F32),
            jax.ShapeDtypeStruct((rows, QKV_W), MM_DTYPE),
            jax.ShapeDtypeStruct((rows, HG_IN_W), F32),
            jax.ShapeDtypeStruct((rows, GATE_W), F32),
        ],
        compiler_params=_params(("parallel",)),
        name="in_proj",
    )(h2d, g, w)


def _s5_operators(a_re, a_im, log_dt, b_re, b_im, c_re, c_im):
    T, G, P, C = S5_T, S5_GROUPS, S5_STATE, S5_GROUP
    hi = lax.Precision.HIGHEST
    dt = jnp.exp(log_dt)[..., None]
    mag = jnp.exp(a_re * dt)
    lr = mag * jnp.cos(a_im * dt)
    li = mag * jnp.sin(a_im * dt)
    den = a_re * a_re + a_im * a_im
    nr = lr - 1.0
    z_re = (nr * a_re + li * a_im) / den
    z_im = (li * a_re - nr * a_im) / den
    bb_re = z_re[..., None] * b_re - z_im[..., None] * b_im
    bb_im = z_re[..., None] * b_im + z_im[..., None] * b_re
    d = jnp.arange(T + 1, dtype=F32)
    pmag = jnp.exp((a_re * dt)[..., None] * d)
    pr = pmag * jnp.cos((a_im * dt)[..., None] * d)
    pi = pmag * jnp.sin((a_im * dt)[..., None] * d)
    lb_re = pr[..., None] * bb_re[:, :, :, None, :] - pi[..., None] * bb_im[:, :, :, None, :]
    lb_im = pr[..., None] * bb_im[:, :, :, None, :] + pi[..., None] * bb_re[:, :, :, None, :]
    cl_re = c_re[..., None] * pr[:, :, None] - c_im[..., None] * pi[:, :, None]
    cl_im = c_re[..., None] * pi[:, :, None] + c_im[..., None] * pr[:, :, None]
    kk = (jnp.einsum('xgop,xgpdi->xgdoi', c_re, lb_re, precision=hi)
          - jnp.einsum('xgop,xgpdi->xgdoi', c_im, lb_im, precision=hi))[:, :, :T]
    kf, kb = kk[0], kk[1]
    kall = jnp.concatenate([kb[:, :0:-1], (kf[:, :1] + kb[:, :1]), kf[:, 1:]], axis=1)
    kall_t = kall.transpose(0, 3, 1, 2).reshape(G, C, (2 * T - 1) * C)

    def pack_cols(re, im):
        z = jnp.zeros_like(re)
        odd = (jnp.arange(G) % 2 == 1)[:, None, None]
        return jnp.concatenate([jnp.where(odd, z, re), jnp.where(odd, re, z),
                                jnp.where(odd, z, im), jnp.where(odd, im, z)], axis=-1)

    vf_re = lb_re[0][:, :, T - 1::-1][:, :, :T].transpose(0, 2, 3, 1).reshape(G, T * C, P)
    vf_im = lb_im[0][:, :, T - 1::-1][:, :, :T].transpose(0, 2, 3, 1).reshape(G, T * C, P)
    vb_re = lb_re[1][:, :, :T].transpose(0, 2, 3, 1).reshape(G, T * C, P)
    vb_im = lb_im[1][:, :, :T].transpose(0, 2, 3, 1).reshape(G, T * C, P)
    v_f = pack_cols(vf_re, vf_im)
    v_b = pack_cols(vb_re, vb_im)
    wf_re = cl_re[0][..., 1:T + 1].transpose(0, 2, 3, 1).reshape(G, P, T * C)
    wf_im = cl_im[0][..., 1:T + 1].transpose(0, 2, 3, 1).reshape(G, P, T * C)
    wb_re = cl_re[1][..., T:0:-1].transpose(0, 2, 3, 1).reshape(G, P, T * C)
    wb_im = cl_im[1][..., T:0:-1].transpose(0, 2, 3, 1).reshape(G, P, T * C)
    pack_rows = lambda re, im: pack_cols(re.transpose(0, 2, 1), -im.transpose(0, 2, 1)).transpose(0, 2, 1)
    w_f = pack_rows(wf_re, wf_im)
    w_b = pack_rows(wb_re, wb_im)
    lam = jnp.stack([pr[0, :, :, T], pi[0, :, :, T], pr[1, :, :, T], pi[1, :, :, T]], axis=0)
    lam = lam.reshape(4, G // 2, 2 * P).transpose(1, 0, 2)
    return kall_t, _mm(v_f), _mm(v_b), _mm(w_f), _mm(w_b), lam


def _s5_pack_body(u_ref, o_ref):
    nbb = o_ref.shape[1]
    per_vreg = 128 // S5_GROUP
    for j in range(S5_T // per_vreg):
        slabs = [u_ref[pl.ds(j * per_vreg + t, nbb, stride=S5_T), :] for t in range(per_vreg)]
        for g in range(per_vreg):
            blk = jnp.concatenate([s[:, g * S5_GROUP:(g + 1) * S5_GROUP] for s in slabs], axis=1)
            o_ref[g, :, j * 128:(j + 1) * 128] = blk.astype(o_ref.dtype)


def _s5_unpack_body(y_ref, o_ref):
    nbb = y_ref.shape[1]
    per_vreg = 128 // S5_GROUP
    for j in range(S5_T // per_vreg):
        blks = [y_ref[g, :, j * 128:(j + 1) * 128] for g in range(per_vreg)]
        for t in range(per_vreg):
            slab = jnp.concatenate([b[:, t * S5_GROUP:(t + 1) * S5_GROUP] for b in blks], axis=1)
            o_ref[pl.ds(j * per_vreg + t, nbb, stride=S5_T), :] = slab


def _s5_inc_body(u_ref, vf_ref, vb_ref, incf_ref, incb_ref):
    incf_ref[...] = _dot(u_ref[0], vf_ref[0]) + _dot(u_ref[1], vf_ref[1])
    incb_ref[...] = _dot(u_ref[0], vb_ref[0]) + _dot(u_ref[1], vb_ref[1])


def _s5_scan_body(ifr_ref, ifi_ref, ibr_ref, ibi_ref, lam_ref, xfr_ref, xfi_ref, xbr_ref, xbi_ref, *, bsz, nc):
    lam = lam_ref[0]
    zero = jnp.zeros((bsz, 2 * S5_STATE), F32)
    m = nc - 1

    def run(inc_re, inc_im, x_re, x_im, lr, li, reverse):
        def step(i, carry):
            if reverse:
                j = jnp.where(i == m, m, m - 1 - i)
            else:
                j = jnp.where(i == 0, m, i - 1)
            rows = pl.ds(j, bsz, stride=nc)
            xr, xi = carry
            x_re[rows, :] = xr
            x_im[rows, :] = xi
            return (lr * xr - li * xi + inc_re[rows, :], lr * xi + li * xr + inc_im[rows, :])

        lax.fori_loop(0, nc, step, (zero, zero))

    run(ifr_ref, ifi_ref, xfr_ref, xfi_ref, lam[0:1], lam[1:2], False)
    run(ibr_ref, ibi_ref, xbr_ref, xbi_ref, lam[2:3], lam[3:4], True)


def _s5_out_body(u_ref, kall_ref, wf_ref, wb_ref, xfr_ref, xfi_ref, xbr_ref, xbi_ref, y_ref, m_ref):
    xf = _mm(jnp.concatenate([xfr_ref[...], xfi_ref[...]], axis=1))
    xb = _mm(jnp.concatenate([xbr_ref[...], xbi_ref[...]], axis=1))
    tw = S5_T * S5_GROUP
    for k in range(2):
        kall = kall_ref[k]
        for s in range(S5_T):
            off = (S5_T - 1 - s) * S5_GROUP
            m_ref[k, s * S5_GROUP:(s + 1) * S5_GROUP, :] = kall[:, off:off + tw].astype(m_ref.dtype)
        y_ref[k] = _dot(u_ref[k], m_ref[k]) + _dot(xf, wf_ref[k]) + _dot(xb, wb_ref[k])


def _s5_scan(u_a, ops, bsz, n_tok):
    kall_t, v_f, v_b, w_f, w_b, lam = ops
    T, G, C = S5_T, S5_GROUPS, S5_GROUP
    l_pad = n_tok + BLK
    nc = l_pad // T
    rows = nc * bsz
    tw = T * C
    sw = 4 * S5_STATE
    nbb = S5_PACK_CHUNKS
    gh = 128 // C
    tok_half = pl.BlockSpec((nbb * T, 128), lambda i, h: (i, h))
    grp_half = pl.BlockSpec((gh, nbb, tw), lambda i, h: (h, i, 0))
    u = pl.pallas_call(
        _s5_pack_body,
        grid=(pl.cdiv(rows, nbb), G // gh),
        in_specs=[tok_half],
        out_specs=grp_half,
        out_shape=jax.ShapeDtypeStruct((G, rows, tw), MM_DTYPE),
        compiler_params=_params(("parallel", "parallel")),
        name="s5_pack",
    )(u_a)
    pair3 = lambda r, c: pl.BlockSpec((2, r, c), lambda p: (p, 0, 0))
    lane = lambda r: pl.BlockSpec((r, sw), lambda p: (0, p))
    re = pl.BlockSpec((rows, sw // 2), lambda p: (0, 2 * p))
    im = pl.BlockSpec((rows, sw // 2), lambda p: (0, 2 * p + 1))
    half = pl.BlockSpec((rows, sw // 2), lambda p: (0, p))
    inc_f, inc_b = pl.pallas_call(
        _s5_inc_body,
        grid=(G // 2,),
        in_specs=[pair3(rows, tw), pair3(tw, sw), pair3(tw, sw)],
        out_specs=[lane(rows), lane(rows)],
        out_shape=[jax.ShapeDtypeStruct((rows, sw * G // 2), F32)] * 2,
        compiler_params=_params(("parallel",)),
        name="s5_inc",
    )(u, v_f, v_b)
    states = pl.pallas_call(
        functools.partial(_s5_scan_body, bsz=bsz, nc=nc),
        grid=(G // 2,),
        in_specs=[re, im, re, im, pl.BlockSpec((1, 4, 2 * S5_STATE), lambda p: (p, 0, 0))],
        out_specs=[half] * 4,
        out_shape=[jax.ShapeDtypeStruct((rows, sw * G // 4), F32)] * 4,
        compiler_params=_params(("parallel",)),
        name="s5_chunk_scan",
    )(inc_f, inc_f, inc_b, inc_b, lam)
    y = pl.pallas_call(
        _s5_out_body,
        grid=(G // 2,),
        in_specs=[pair3(rows, tw), pair3(C, (2 * T - 1) * C), pair3(sw, tw), pair3(sw, tw)] + [half] * 4,
        out_specs=pair3(rows, tw),
        out_shape=jax.ShapeDtypeStruct((G, rows, tw), F32),
        scratch_shapes=[pltpu.VMEM((2, tw, tw), MM_DTYPE)],
        compiler_params=_params(("parallel",)),
        name="s5_out",
    )(u, kall_t, w_f, w_b, *states)
    return pl.pallas_call(
        _s5_unpack_body,
        grid=(pl.cdiv(rows, nbb), G // gh),
        in_specs=[grp_half],
        out_specs=tok_half,
        out_shape=jax.ShapeDtypeStruct((rows * T, S5_WIDTH), F32),
        compiler_params=_params(("parallel", "parallel")),
        name="s5_unpack",
    )(y)


def _na_bias(rpb):
    cols = np.arange(GRID_W)
    wstart = np.clip(cols - NA_COLS // 2, 0, GRID_W - NA_COLS)
    inwin = (cols[None, :] >= wstart[:, None]) & (cols[None, :] < wstart[:, None] + NA_COLS)
    dcol = np.clip(cols[None, :] - cols[:, None], -(NA_COLS - 1), NA_COLS - 1) + NA_COLS - 1
    drow = np.arange(NA_ROWS)[None, :] - np.arange(NA_ROWS)[:, None] + NA_ROWS - 1
    b = rpb.astype(F32)[:, drow][:, :, :, dcol]
    b = jnp.where(inwin[None, None, None], b, NEG)
    b = b.transpose(1, 0, 3, 2, 4)
    return b.reshape(NA_ROWS, NA_HEADS // NA_HG, NA_HG * GRID_W, NA_ROWS * GRID_W)


def _head_select(o, rows_per_head, lane_head):
    out = jnp.where(lane_head == 0, o[0:rows_per_head], 0.0)
    for h in range(1, NA_HG):
        out = out + jnp.where(lane_head == h, o[h * rows_per_head:(h + 1) * rows_per_head], 0.0)
    return out


def _na_body(q_ref, k_ref, v_ref, bias_ref, o_ref, *, n_tok):
    rg = pl.program_id(2)
    n_rg = n_tok // (NA_QROWS * GRID_W)
    grid_rows = n_tok // GRID_W
    meta0 = n_tok + PAD
    scale = NA_HEAD_DIM ** -0.5
    hw = NA_HG * NA_HEAD_DIM
    km = k_ref[0, meta0:meta0 + N_META, :]
    vm = v_ref[0, meta0:meta0 + N_META, :]

    def masked_queries(q, nq):
        row_head = lax.broadcasted_iota(jnp.int32, (NA_HG * nq, hw), 0) // nq
        lane_head = lax.broadcasted_iota(jnp.int32, (NA_HG * nq, hw), 1) // NA_HEAD_DIM
        qq = jnp.concatenate([q] * NA_HG, axis=0)
        return jnp.where(row_head == lane_head, qq, jnp.zeros_like(qq))

    @pl.when(rg < n_rg)
    def _grid_rows():
        lane_head = lax.broadcasted_iota(jnp.int32, (GRID_W, hw), 1) // NA_HEAD_DIM

        def row_step(rr, carry):
            r = rg * NA_QROWS + rr
            rs = jnp.clip(r - NA_ROWS // 2, 0, grid_rows - NA_ROWS)
            q0 = pl.multiple_of(rr * GRID_W, GRID_W)
            k0 = pl.multiple_of(rs * GRID_W, GRID_W)
            q = q_ref[0, pl.ds(q0, GRID_W), :] * scale
            qm = masked_queries(q.astype(q_ref.dtype), GRID_W)
            ks = k_ref[0, pl.ds(k0, NA_ROWS * GRID_W), :]
            vs = v_ref[0, pl.ds(k0, NA_ROWS * GRID_W), :]
            s = _dot_nt(qm, ks) + bias_ref[r - rs, 0]
            sm = _dot_nt(qm, km)
            fold = lambda x, op: functools.reduce(op, [x[:, i:i + 128] for i in range(0, x.shape[1], 128)])
            sm_wide = jnp.concatenate([sm, jnp.full((sm.shape[0], 128 - N_META), NEG, F32)], axis=1)
            mx = jnp.max(jnp.maximum(fold(s, jnp.maximum), sm_wide), axis=-1, keepdims=True)
            p = jnp.exp(s - mx)
            pm = jnp.exp(sm_wide - mx)
            den = jnp.sum(fold(p, jnp.add) + pm, axis=-1, keepdims=True)
            o = (_dot(_mm(p), vs) + _dot(_mm(pm[:, 0:N_META]), vm)) / den
            o_ref[0, pl.ds(q0, GRID_W), :] = _head_select(o, GRID_W, lane_head).astype(o_ref.dtype)
            return carry

        lax.fori_loop(0, NA_QROWS, row_step, 0, unroll=True)

    @pl.when(rg == n_rg)
    def _meta_rows():
        lane_head = lax.broadcasted_iota(jnp.int32, (N_META, hw), 1) // NA_HEAD_DIM
        q = q_ref[0, PAD:PAD + N_META, :] * scale
        qm = masked_queries(q.astype(q_ref.dtype), N_META)
        s = _dot_nt(qm, km)
        p = jnp.exp(s - jnp.max(s, axis=-1, keepdims=True))
        o = _dot(_mm(p), vm) / jnp.sum(p, axis=-1, keepdims=True)
        o_ref[0, 0:PAD, :] = jnp.zeros((PAD, hw), o_ref.dtype)
        o_ref[0, PAD:BLK, :] = _head_select(o, N_META, lane_head).astype(o_ref.dtype)


def _neighborhood_attention(qkv, bias, bsz, n_tok):
    l_pad = n_tok + BLK
    hw = NA_HG * NA_HEAD_DIM
    n_hg = NA_HEADS // NA_HG
    qrows = NA_QROWS * GRID_W
    whole = lambda off: pl.BlockSpec((1, l_pad, hw), lambda b, g, r: (b, 0, off + g), pipeline_mode=pl.Buffered(1))
    return pl.pallas_call(
        functools.partial(_na_body, n_tok=n_tok),
        grid=(bsz, n_hg, n_tok // qrows + 1),
        in_specs=[
            pl.BlockSpec((1, qrows, hw), lambda b, g, r: (b, r, g)),
            whole(n_hg),
            whole(2 * n_hg),
            pl.BlockSpec((NA_ROWS, 1, NA_HG * GRID_W, NA_ROWS * GRID_W), lambda b, g, r: (0, g, 0, 0)),
        ],
        out_specs=pl.BlockSpec((1, qrows, hw), lambda b, g, r: (b, r, g)),
        out_shape=jax.ShapeDtypeStruct((bsz, l_pad, NA_WIDTH), MM_DTYPE),
        compiler_params=_params(("parallel", "parallel", "arbitrary")),
        name="neighborhood_attention",
    )(qkv, qkv, qkv, bias)


def _hgrn_masks(rev):
    C, c = BLK, HG_SUB
    i = np.arange(C)
    s = np.arange(C // c)[:, None]
    d = np.arange(c)[:, None]
    keys_ok = (i[None] >= (s + 1) * c) if rev else (i[None] < s * c)
    diag_ok = (i[None] % c + d <= c - 1) if rev else (i[None] % c >= d)
    table = lambda ok: np.broadcast_to(np.where(ok, 0.0, NEG).astype(np.float32)[:, :, None], ok.shape + (HG_WIDTH,))
    return jnp.asarray(table(keys_ok)), jnp.asarray(table(diag_ok))


def _hgrn_chunk(qraw, ff, v, lb, keys_ref, diag_ref, st_ref, rev):
    C, c = BLK, HG_SUB
    nsub = C // c
    w = HG_WIDTH
    g2 = jnp.log2(lb + (1.0 - lb) * jax.nn.sigmoid(ff))
    k = (1.0 - lb) * jax.nn.sigmoid(-ff)
    q = qraw * jax.nn.sigmoid(qraw)
    ri = lax.broadcasted_iota(jnp.int32, (C, C), 0)
    ci = lax.broadcasted_iota(jnp.int32, (C, C), 1)
    tri = jnp.where((ci >= ri) if rev else (ci <= ri), 1.0, 0.0).astype(F32)
    b2 = _dot_f32(tri, g2)
    edge = b2[0:1] if rev else b2[C - 1:C]
    st = st_ref[...]
    acc = _dot_nt(_mm(q * jnp.exp2(b2)), _mm(st))
    d_t = _dot_tn(_mm(v), _mm(k * jnp.exp2(edge - b2)))
    blk_r = lax.broadcasted_iota(jnp.int32, (w, w), 0) // HG_DK
    blk_c = lax.broadcasted_iota(jnp.int32, (w, w), 1) // HG_DK
    same_head = blk_r == blk_c
    st_ref[...] = st * jnp.exp2(edge) + jnp.where(same_head, d_t, 0.0)
    hsum = jnp.where(same_head, 1.0, 0.0).astype(MM_DTYPE)
    vmm = _mm(v)
    row_head = lax.broadcasted_iota(jnp.int32, (HG_HEADS * c, w), 0) // c
    lane_head_q = lax.broadcasted_iota(jnp.int32, (HG_HEADS * c, w), 1) // HG_DK
    lane_head = lax.broadcasted_iota(jnp.int32, (c, w), 1) // HG_DK
    subs = range(nsub - 1) if rev else range(1, nsub)
    atts = []
    for s in subs:
        lo, hi = c * s, c * (s + 1)
        beta = b2[hi:hi + 1] if rev else b2[lo - 1:lo]
        qt = q[lo:hi] * jnp.exp2(b2[lo:hi] - beta)
        kt = k * jnp.exp2((beta - b2) + keys_ref[s])
        qh = jnp.where(row_head == lane_head_q, jnp.concatenate([qt] * HG_HEADS, axis=0), 0.0)
        atts.append(_dot_nt(_mm(qh), _mm(kt)))
    r = _dot(_mm(jnp.concatenate(atts, axis=0)), vmm)
    zero = jnp.zeros((c, w), F32)
    outs = []
    for n, s in enumerate(subs):
        o = zero
        for h in range(HG_HEADS):
            row = (n * HG_HEADS + h) * c
            o = o + jnp.where(lane_head == h, r[row:row + c], 0.0)
        outs.append(o)
    acc = acc + jnp.concatenate(outs + [zero] if rev else [zero] + outs, axis=0)
    a = _dot(_mm(q * k), hsum)
    acc = acc + a * v
    for d in range(1, c):
        shift = C - d if rev else d
        p = q * pltpu.roll(k, shift, 0) * jnp.exp2((b2 - pltpu.roll(b2, shift, 0)) + diag_ref[d])
        acc = acc + _dot(_mm(p), hsum) * pltpu.roll(v, shift, 0)
    return acc


def _hgrn_body(qf_ref, ff_ref, vf_ref, qb_ref, fb_ref, vb_ref, lb_ref, kf_ref, df_ref, kb_ref, db_ref,
               of_ref, ob_ref, stf_ref, stb_ref):
    @pl.when(pl.program_id(1) == 0)
    def _reset():
        stf_ref[...] = jnp.zeros_like(stf_ref)
        stb_ref[...] = jnp.zeros_like(stb_ref)

    lb = lb_ref[...]
    of_ref[0] = _hgrn_chunk(qf_ref[0], ff_ref[0], vf_ref[0], lb, kf_ref, df_ref, stf_ref, False)
    ob_ref[0] = _hgrn_chunk(qb_ref[0], fb_ref[0], vb_ref[0], lb, kb_ref, db_ref, stb_ref, True)


def _hgrn_scan(hg, lb, bsz, n_tok):
    l_pad = n_tok + BLK
    nb = l_pad // BLK
    m = nb - 1
    fwd = lambda n: (n + m) % nb
    bwd = lambda n: jnp.where(n == m, m, m - 1 - n)
    spec = lambda order, col: pl.BlockSpec((1, BLK, HG_WIDTH), lambda b, n: (b, order(n), col))
    masks = _hgrn_masks(False) + _hgrn_masks(True)
    return pl.pallas_call(
        _hgrn_body,
        grid=(bsz, nb),
        in_specs=[spec(fwd, 0), spec(fwd, 1), spec(fwd, 3), spec(bwd, 0), spec(bwd, 2), spec(bwd, 3),
                  pl.BlockSpec((1, HG_WIDTH), lambda b, n: (0, 0))] + [_const_spec(t.shape) for t in masks],
        out_specs=[spec(fwd, 0), spec(bwd, 0)],
        out_shape=[jax.ShapeDtypeStruct((bsz, l_pad, HG_WIDTH), F32)] * 2,
        scratch_shapes=[pltpu.VMEM((HG_WIDTH, HG_WIDTH), F32)] * 2,
        compiler_params=_params(("parallel", "arbitrary")),
        name="hgrn2_scan",
    )(hg, hg, hg, hg, hg, hg, lb, *masks)


def _out_ffn_body(h_ref, ys_ref, ua_ref, att_ref, of_ref, ob_ref, gc_ref, gt_ref,
                  d_ref, wglu_ref, on_ref, wa_ref, wb_ref, wc_ref, wo_ref, n2_ref, wg_ref, wu_ref, wd_ref,
                  out_ref, *, blocks_per_seq):
    tm = h_ref.shape[0]
    y = jax.nn.gelu(ys_ref[...] + d_ref[...] * ua_ref[...])
    y = y * jax.nn.sigmoid(_dot(_mm(y), wglu_ref[...]))
    o = of_ref[...] + ob_ref[...]
    blk_r = lax.broadcasted_iota(jnp.int32, (HG_WIDTH, HG_WIDTH), 0) // HG_DK
    blk_c = lax.broadcasted_iota(jnp.int32, (HG_WIDTH, HG_WIDTH), 1) // HG_DK
    head_mean = jnp.where(blk_r == blk_c, 1.0 / HG_DK, 0.0).astype(F32)
    o = o * lax.rsqrt(_dot_f32(o * o, head_mean) + EPS) * on_ref[...]
    gc = gc_ref[...]
    o = o * (gc * jax.nn.sigmoid(gc))
    mix = (jax.nn.sigmoid(gt_ref[:, 0:D_MODEL]) * _dot(_mm(y), wa_ref[...])
           + jax.nn.sigmoid(gt_ref[:, D_MODEL:2 * D_MODEL]) * _dot(att_ref[...], wb_ref[...])
           + jax.nn.sigmoid(gt_ref[:, 2 * D_MODEL:3 * D_MODEL]) * _dot(_mm(o), wc_ref[...]))
    h = h_ref[...] + _dot(_mm(mix), wo_ref[...])
    hn = _mm(h * lax.rsqrt(jnp.mean(h * h, axis=-1, keepdims=True) + EPS) * n2_ref[...])
    for c0 in range(0, FFN_HIDDEN, FFN_CHUNK):
        gate = _dot(hn, wg_ref[:, c0:c0 + FFN_CHUNK])
        up = _dot(hn, wu_ref[:, c0:c0 + FFN_CHUNK])
        h = h + _dot(_mm(gate * jax.nn.sigmoid(gate) * up), wd_ref[c0:c0 + FFN_CHUNK, :])
    out_ref[...] = h
    first_blk = pl.program_id(0) * (tm // BLK)
    for j in range(tm // BLK):
        @pl.when((first_blk + j) % blocks_per_seq == blocks_per_seq - 1)
        def _zero_rows():
            out_ref[j * BLK:j * BLK + PAD, :] = jnp.zeros((PAD, D_MODEL), F32)


def _out_ffn(h2d, ys, ua, att, o_f, o_b, hg, gates, lw, blocks_per_seq):
    rows = h2d.shape[0]
    tm = ROW_TILE_OUT
    row = lambda width, col=0: pl.BlockSpec((tm, width), lambda i: (i, col))
    weights = [lw["d"], lw["w_glu"], lw["onorm"], lw["w_up_a"], lw["w_up_b"], lw["w_up_c"], lw["w_o"],
               lw["norm2"], lw["w_ffn_gate"], lw["w_ffn_up"], lw["w_ffn_down"]]
    return pl.pallas_call(
        functools.partial(_out_ffn_body, blocks_per_seq=blocks_per_seq),
        grid=(pl.cdiv(rows, tm),),
        in_specs=[row(D_MODEL), row(S5_WIDTH), row(S5_WIDTH), row(NA_WIDTH), row(HG_WIDTH), row(HG_WIDTH),
                  row(HG_WIDTH, 4), row(GATE_W)] + [_const_spec(w.shape) for w in weights],
        out_specs=row(D_MODEL),
        out_shape=jax.ShapeDtypeStruct((rows, D_MODEL), F32),
        compiler_params=_params(("parallel",)),
        name="mix_out_ffn",
    )(h2d, ys, ua, att, o_f, o_b, hg, gates, *weights)


def _final_norm_body(h_ref, g_ref, o_ref):
    x = h_ref[0]
    o_ref[0] = x * lax.rsqrt(jnp.mean(x * x, axis=-1, keepdims=True) + EPS) * g_ref[...]


def _final_norm(h, g, n_tok):
    bsz = h.shape[0]
    tl = 512
    spec = pl.BlockSpec((1, tl, D_MODEL), lambda b, j: (b, j, 0))
    return pl.pallas_call(
        _final_norm_body,
        grid=(bsz, n_tok // tl),
        in_specs=[spec, pl.BlockSpec((1, D_MODEL), lambda b, j: (0, 0))],
        out_specs=spec,
        out_shape=jax.ShapeDtypeStruct((bsz, n_tok, D_MODEL), F32),
        compiler_params=_params(("parallel", "parallel")),
        name="final_norm",
    )(h, g)


def _trunk(x, meta_tokens, layers, final_g):
    bsz, n_tok, _ = x.shape
    l_pad = n_tok + BLK
    rows = bsz * l_pad
    meta = jnp.broadcast_to(meta_tokens[None].astype(F32), (bsz, N_META, D_MODEL))
    h = jnp.concatenate([x.astype(F32), jnp.zeros((bsz, PAD, D_MODEL), F32), meta], axis=1).reshape(rows, D_MODEL)
    for lw in layers:
        ua, qkv, hg, gates = _in_proj(h, lw["norm1"], lw["w_in"])
        ys = _s5_scan(ua, lw["s5_ops"], bsz, n_tok)
        att = _neighborhood_attention(qkv.reshape(bsz, l_pad, QKV_W), lw["na_bias"], bsz, n_tok)
        o_f, o_b = _hgrn_scan(hg.reshape(bsz, l_pad, HG_IN_W), lw["lb"], bsz, n_tok)
        h = _out_ffn(h, ys, ua, att.reshape(rows, NA_WIDTH), o_f.reshape(rows, HG_WIDTH),
                     o_b.reshape(rows, HG_WIDTH), hg, gates, lw, l_pad // BLK)
    return _final_norm(h.reshape(bsz, l_pad, D_MODEL), final_g, n_tok)


def kernel(x_prompt, x_sample, meta_tokens, norm1_g, w_in, s5_a_re, s5_a_im, s5_log_dt, s5_b_re, s5_b_im,
           s5_c_re, s5_c_im, s5_d, s5_w_glu, na_rpb, hg_lb_logits, hg_onorm_g, w_up_a, w_up_b, w_up_c,
           w_o, norm2_g, w_ffn_gate, w_ffn_up, w_ffn_down, final_norm_g):
    depth = w_in.shape[0]
    sm = jax.nn.softmax(hg_lb_logits.astype(F32), axis=0)
    lbs = jnp.cumsum(sm, axis=0) - sm[0:1]
    layers = []
    for l in range(depth):
        layers.append(dict(
            norm1=norm1_g[l].astype(F32)[None],
            w_in=_mm(w_in[l]),
            s5_ops=_s5_operators(s5_a_re[l].astype(F32), s5_a_im[l].astype(F32), s5_log_dt[l].astype(F32),
                                 s5_b_re[l].astype(F32), s5_b_im[l].astype(F32), s5_c_re[l].astype(F32),
                                 s5_c_im[l].astype(F32)),
            na_bias=_na_bias(na_rpb[l]),
            lb=lbs[l][None],
            d=s5_d[l].astype(F32)[None],
            w_glu=_mm(s5_w_glu[l]),
            onorm=jnp.tile(hg_onorm_g[l].astype(F32), HG_HEADS)[None],
            w_up_a=_mm(w_up_a[l]), w_up_b=_mm(w_up_b[l]), w_up_c=_mm(w_up_c[l]), w_o=_mm(w_o[l]),
            norm2=norm2_g[l].astype(F32)[None],
            w_ffn_gate=_mm(w_ffn_gate[l]), w_ffn_up=_mm(w_ffn_up[l]), w_ffn_down=_mm(w_ffn_down[l]),
        ))
    final_g = final_norm_g.astype(F32)[None]
    return (_trunk(x_prompt, meta_tokens, layers, final_g), _trunk(x_sample, meta_tokens, layers, final_g))
```

```python
import functools

import numpy as np
import jax
import jax.numpy as jnp
from jax import lax
from jax.experimental import pallas as pl
from jax.experimental.pallas import tpu as pltpu

F32 = jnp.float32
MM_DTYPE = jnp.bfloat16

D_MODEL = 1024
N_META = 16
GRID_W = 64
EPS = 1e-6
NEG = -1e30
BLK = 64
PAD = BLK - N_META
S5_WIDTH = 256
S5_GROUP = 16
S5_GROUPS = 16
S5_STATE = 64
S5_T = 64
S5_PACK_CHUNKS = 16
NA_HEADS = 8
NA_HEAD_DIM = 64
NA_WIDTH = 512
NA_ROWS = 8
NA_COLS = 16
NA_HG = 4
NA_QROWS = 8
HG_HEADS = 4
HG_DK = 64
HG_WIDTH = 256
HG_SUB = 8
HG_BATCH = 4
FFN_HIDDEN = 2816
FFN_CHUNK = 2816
IN_WIDTH = 6144
QKV_W = 3 * NA_WIDTH
HG_IN_W = 7 * HG_WIDTH
GATE_W = 3 * D_MODEL
ROW_TILE_IN = 256
ROW_TILE_OUT = 256
VMEM_LIMIT = 56 * 1024 * 1024


def _dot(a, b):
    return jnp.dot(a, b, preferred_element_type=F32)


def _dot_nt(a, b):
    return lax.dot_general(a, b, (((1,), (1,)), ((), ())), preferred_element_type=F32)


def _dot_tn(a, b):
    return lax.dot_general(a, b, (((0,), (0,)), ((), ())), preferred_element_type=F32)


def _dot_f32(a, b):
    return jnp.dot(a, b, preferred_element_type=F32, precision=lax.Precision.HIGHEST)


def _mm(x):
    return x.astype(MM_DTYPE)


def _sigmoid(x):
    return 0.5 * jnp.tanh(0.5 * x) + 0.5


def _head_block_mask(dtype, scale=1.0):
    r = np.arange(HG_WIDTH) // HG_DK
    return jnp.asarray(np.where(r[:, None] == r[None, :], scale, 0.0), dtype)


def _const_spec(shape):
    nd = len(shape)
    return pl.BlockSpec(shape, lambda *_: (0,) * nd, pipeline_mode=pl.Buffered(1))


def _params(sem):
    return pltpu.CompilerParams(dimension_semantics=sem, vmem_limit_bytes=VMEM_LIMIT)


def _in_proj_body(h_ref, g_ref, w_ref, lb_ref, ua_ref, qkv_ref, hg_ref, gt_ref):
    x = h_ref[...]
    xn = x * lax.rsqrt(jnp.mean(x * x, axis=-1, keepdims=True) + EPS) * g_ref[...]
    xn = _mm(xn)
    w_hg = HG_WIDTH
    col = 0
    for ref, width in ((ua_ref, S5_WIDTH), (qkv_ref, QKV_W)):
        ref[...] = _dot(xn, w_ref[:, col:col + width]).astype(ref.dtype)
        col += width
    z = _dot(xn, w_ref[:, col:col + 5 * w_hg])
    col += 5 * w_hg
    q = z[:, 0:w_hg]
    hg_ref[:, 0:w_hg] = q * _sigmoid(q)
    hg_ref[:, 3 * w_hg:5 * w_hg] = z[:, 3 * w_hg:5 * w_hg]
    lb = lb_ref[...]
    for src, dst in ((1, 5), (2, 6)):
        sig = jax.nn.sigmoid(z[:, src * w_hg:(src + 1) * w_hg])
        hg_ref[:, src * w_hg:(src + 1) * w_hg] = jnp.log2(lb + (1.0 - lb) * sig)
        hg_ref[:, dst * w_hg:(dst + 1) * w_hg] = jnp.log2((1.0 - lb) * (1.0 - sig))
    gt_ref[...] = _dot(xn, w_ref[:, col:col + GATE_W])


def _in_proj(h2d, g, w, lb):
    rows = h2d.shape[0]
    tm = ROW_TILE_IN
    row = lambda width: pl.BlockSpec((tm, width), lambda i: (i, 0))
    return pl.pallas_call(
        _in_proj_body,
        grid=(pl.cdiv(rows, tm),),
        in_specs=[row(D_MODEL), _const_spec((1, D_MODEL)), _const_spec((D_MODEL, IN_WIDTH)), _const_spec((1, HG_WIDTH))],
        out_specs=[row(S5_WIDTH), row(QKV_W), row(HG_IN_W), row(GATE_W)],
        out_shape=[
            jax.ShapeDtypeStruct((rows, S5_WIDTH), F32),
            jax.ShapeDtypeStruct((rows, QKV_W), MM_DTYPE),
            jax.ShapeDtypeStruct((rows, HG_IN_W), F32),
            jax.ShapeDtypeStruct((rows, GATE_W), F32),
        ],
        compiler_params=_params(("parallel",)),
        name="in_proj",
    )(h2d, g, w, lb)


def _s5_operators(a_re, a_im, log_dt, b_re, b_im, c_re, c_im):
    T, G, P, C = S5_T, S5_GROUPS, S5_STATE, S5_GROUP
    hi = lax.Precision.HIGHEST
    dt = jnp.exp(log_dt)[..., None]
    mag = jnp.exp(a_re * dt)
    lr = mag * jnp.cos(a_im * dt)
    li = mag * jnp.sin(a_im * dt)
    den = a_re * a_re + a_im * a_im
    nr = lr - 1.0
    z_re = (nr * a_re + li * a_im) / den
    z_im = (li * a_re - nr * a_im) / den
    bb_re = z_re[..., None] * b_re - z_im[..., None] * b_im
    bb_im = z_re[..., None] * b_im + z_im[..., None] * b_re
    d = jnp.arange(T + 1, dtype=F32)
    pmag = jnp.exp((a_re * dt)[..., None] * d)
    pr = pmag * jnp.cos((a_im * dt)[..., None] * d)
    pi = pmag * jnp.sin((a_im * dt)[..., None] * d)
    lb_re = pr[..., None] * bb_re[:, :, :, None, :] - pi[..., None] * bb_im[:, :, :, None, :]
    lb_im = pr[..., None] * bb_im[:, :, :, None, :] + pi[..., None] * bb_re[:, :, :, None, :]
    cl_re = c_re[..., None] * pr[:, :, None] - c_im[..., None] * pi[:, :, None]
    cl_im = c_re[..., None] * pi[:, :, None] + c_im[..., None] * pr[:, :, None]
    kk = (jnp.einsum('xgop,xgpdi->xgdoi', c_re, lb_re, precision=hi)
          - jnp.einsum('xgop,xgpdi->xgdoi', c_im, lb_im, precision=hi))[:, :, :T]
    kf, kb = kk[0], kk[1]
    kall = jnp.concatenate([kb[:, :0:-1], (kf[:, :1] + kb[:, :1]), kf[:, 1:]], axis=1)
    kall_t = kall.transpose(0, 3, 1, 2).reshape(G, C, (2 * T - 1) * C)

    def pack_cols(re, im):
        z = jnp.zeros_like(re)
        odd = (jnp.arange(G) % 2 == 1)[:, None, None]
        return jnp.concatenate([jnp.where(odd, z, re), jnp.where(odd, re, z),
                                jnp.where(odd, z, im), jnp.where(odd, im, z)], axis=-1)

    vf_re = lb_re[0][:, :, T - 1::-1][:, :, :T].transpose(0, 2, 3, 1).reshape(G, T * C, P)
    vf_im = lb_im[0][:, :, T - 1::-1][:, :, :T].transpose(0, 2, 3, 1).reshape(G, T * C, P)
    vb_re = lb_re[1][:, :, :T].transpose(0, 2, 3, 1).reshape(G, T * C, P)
    vb_im = lb_im[1][:, :, :T].transpose(0, 2, 3, 1).reshape(G, T * C, P)
    v_f = pack_cols(vf_re, vf_im)
    v_b = pack_cols(vb_re, vb_im)
    wf_re = cl_re[0][..., 1:T + 1].transpose(0, 2, 3, 1).reshape(G, P, T * C)
    wf_im = cl_im[0][..., 1:T + 1].transpose(0, 2, 3, 1).reshape(G, P, T * C)
    wb_re = cl_re[1][..., T:0:-1].transpose(0, 2, 3, 1).reshape(G, P, T * C)
    wb_im = cl_im[1][..., T:0:-1].transpose(0, 2, 3, 1).reshape(G, P, T * C)
    pack_rows = lambda re, im: pack_cols(re.transpose(0, 2, 1), -im.transpose(0, 2, 1)).transpose(0, 2, 1)
    w_f = pack_rows(wf_re, wf_im)
    w_b = pack_rows(wb_re, wb_im)
    lam = jnp.stack([pr[0, :, :, T], pi[0, :, :, T], pr[1, :, :, T], pi[1, :, :, T]], axis=0)
    lam = lam.reshape(4, G // 2, 2 * P).transpose(1, 0, 2)
    return kall_t, _mm(v_f), _mm(v_b), _mm(w_f), _mm(w_b), lam


def _s5_pack_body(u_ref, o_ref):
    nbb = o_ref.shape[1]
    per_vreg = 128 // S5_GROUP
    for j in range(S5_T // per_vreg):
        slabs = [u_ref[pl.ds(j * per_vreg + t, nbb, stride=S5_T), :] for t in range(per_vreg)]
        for g in range(per_vreg):
            blk = jnp.concatenate([s[:, g * S5_GROUP:(g + 1) * S5_GROUP] for s in slabs], axis=1)
            o_ref[g, :, j * 128:(j + 1) * 128] = blk.astype(o_ref.dtype)


def _s5_unpack_body(y_ref, o_ref):
    nbb = y_ref.shape[1]
    per_vreg = 128 // S5_GROUP
    for j in range(S5_T // per_vreg):
        blks = [y_ref[g, :, j * 128:(j + 1) * 128] for g in range(per_vreg)]
        for t in range(per_vreg):
            slab = jnp.concatenate([b[:, t * S5_GROUP:(t + 1) * S5_GROUP] for b in blks], axis=1)
            o_ref[pl.ds(j * per_vreg + t, nbb, stride=S5_T), :] = slab


def _s5_inc_body(u_ref, vf_ref, vb_ref, incf_ref, incb_ref):
    incf_ref[...] = _dot(u_ref[0], vf_ref[0]) + _dot(u_ref[1], vf_ref[1])
    incb_ref[...] = _dot(u_ref[0], vb_ref[0]) + _dot(u_ref[1], vb_ref[1])


def _s5_scan_body(ifr_ref, ifi_ref, ibr_ref, ibi_ref, lam_ref, xfr_ref, xfi_ref, xbr_ref, xbi_ref, *, bsz, nc):
    lam = lam_ref[0]
    zero = jnp.zeros((bsz, 2 * S5_STATE), F32)
    m = nc - 1

    def run(inc_re, inc_im, x_re, x_im, lr, li, reverse):
        def step(i, carry):
            if reverse:
                j = jnp.where(i == m, m, m - 1 - i)
            else:
                j = jnp.where(i == 0, m, i - 1)
            rows = pl.ds(j, bsz, stride=nc)
            xr, xi = carry
            x_re[rows, :] = xr
            x_im[rows, :] = xi
            return (lr * xr - li * xi + inc_re[rows, :], lr * xi + li * xr + inc_im[rows, :])

        lax.fori_loop(0, nc, step, (zero, zero))

    run(ifr_ref, ifi_ref, xfr_ref, xfi_ref, lam[0:1], lam[1:2], False)
    run(ibr_ref, ibi_ref, xbr_ref, xbi_ref, lam[2:3], lam[3:4], True)


def _s5_out_body(u_ref, kall_ref, wf_ref, wb_ref, xfr_ref, xfi_ref, xbr_ref, xbi_ref, y_ref, m_ref):
    xf = _mm(jnp.concatenate([xfr_ref[...], xfi_ref[...]], axis=1))
    xb = _mm(jnp.concatenate([xbr_ref[...], xbi_ref[...]], axis=1))
    tw = S5_T * S5_GROUP
    for k in range(2):
        kall = kall_ref[k]
        for s in range(S5_T):
            off = (S5_T - 1 - s) * S5_GROUP
            m_ref[k, s * S5_GROUP:(s + 1) * S5_GROUP, :] = kall[:, off:off + tw].astype(m_ref.dtype)
        y_ref[k] = _dot(u_ref[k], m_ref[k]) + _dot(xf, wf_ref[k]) + _dot(xb, wb_ref[k])


def _s5_scan(u_a, ops, bsz, n_tok):
    kall_t, v_f, v_b, w_f, w_b, lam = ops
    T, G, C = S5_T, S5_GROUPS, S5_GROUP
    l_pad = n_tok + BLK
    nc = l_pad // T
    rows = nc * bsz
    tw = T * C
    sw = 4 * S5_STATE
    nbb = S5_PACK_CHUNKS
    gh = 128 // C
    tok_half = pl.BlockSpec((nbb * T, 128), lambda i, h: (i, h))
    grp_half = pl.BlockSpec((gh, nbb, tw), lambda i, h: (h, i, 0))
    u = pl.pallas_call(
        _s5_pack_body,
        grid=(pl.cdiv(rows, nbb), G // gh),
        in_specs=[tok_half],
        out_specs=grp_half,
        out_shape=jax.ShapeDtypeStruct((G, rows, tw), MM_DTYPE),
        compiler_params=_params(("parallel", "parallel")),
        name="s5_pack",
    )(u_a)
    pair3 = lambda r, c: pl.BlockSpec((2, r, c), lambda p: (p, 0, 0))
    lane = lambda r: pl.BlockSpec((r, sw), lambda p: (0, p))
    re = pl.BlockSpec((rows, sw // 2), lambda p: (0, 2 * p))
    im = pl.BlockSpec((rows, sw // 2), lambda p: (0, 2 * p + 1))
    half = pl.BlockSpec((rows, sw // 2), lambda p: (0, p))
    inc_f, inc_b = pl.pallas_call(
        _s5_inc_body,
        grid=(G // 2,),
        in_specs=[pair3(rows, tw), pair3(tw, sw), pair3(tw, sw)],
        out_specs=[lane(rows), lane(rows)],
        out_shape=[jax.ShapeDtypeStruct((rows, sw * G // 2), F32)] * 2,
        compiler_params=_params(("parallel",)),
        name="s5_inc",
    )(u, v_f, v_b)
    states = pl.pallas_call(
        functools.partial(_s5_scan_body, bsz=bsz, nc=nc),
        grid=(G // 2,),
        in_specs=[re, im, re, im, pl.BlockSpec((1, 4, 2 * S5_STATE), lambda p: (p, 0, 0))],
        out_specs=[half] * 4,
        out_shape=[jax.ShapeDtypeStruct((rows, sw * G // 4), F32)] * 4,
        compiler_params=_params(("parallel",)),
        name="s5_chunk_scan",
    )(inc_f, inc_f, inc_b, inc_b, lam)
    y = pl.pallas_call(
        _s5_out_body,
        grid=(G // 2,),
        in_specs=[pair3(rows, tw), pair3(C, (2 * T - 1) * C), pair3(sw, tw), pair3(sw, tw)] + [half] * 4,
        out_specs=pair3(rows, tw),
        out_shape=jax.ShapeDtypeStruct((G, rows, tw), F32),
        scratch_shapes=[pltpu.VMEM((2, tw, tw), MM_DTYPE)],
        compiler_params=_params(("parallel",)),
        name="s5_out",
    )(u, kall_t, w_f, w_b, *states)
    return pl.pallas_call(
        _s5_unpack_body,
        grid=(pl.cdiv(rows, nbb), G // gh),
        in_specs=[grp_half],
        out_specs=tok_half,
        out_shape=jax.ShapeDtypeStruct((rows * T, S5_WIDTH), F32),
        compiler_params=_params(("parallel", "parallel")),
        name="s5_unpack",
    )(y)


def _na_bias(rpb):
    cols = np.arange(GRID_W)
    wstart = np.clip(cols - NA_COLS // 2, 0, GRID_W - NA_COLS)
    inwin = (cols[None, :] >= wstart[:, None]) & (cols[None, :] < wstart[:, None] + NA_COLS)
    dcol = np.clip(cols[None, :] - cols[:, None], -(NA_COLS - 1), NA_COLS - 1) + NA_COLS - 1
    drow = np.arange(NA_ROWS)[None, :] - np.arange(NA_ROWS)[:, None] + NA_ROWS - 1
    b = rpb.astype(F32)[:, drow][:, :, :, dcol]
    b = jnp.where(inwin[None, None, None], b, NEG)
    b = b.transpose(1, 0, 3, 2, 4)
    return b.reshape(NA_ROWS, NA_HEADS // NA_HG, NA_HG * GRID_W, NA_ROWS * GRID_W)


def _head_select(o, rows_per_head, lane_head):
    out = jnp.where(lane_head == 0, o[0:rows_per_head], 0.0)
    for h in range(1, NA_HG):
        out = out + jnp.where(lane_head == h, o[h * rows_per_head:(h + 1) * rows_per_head], 0.0)
    return out


def _na_body(q_ref, k_ref, v_ref, bias_ref, o_ref, *, n_tok):
    rg = pl.program_id(2)
    n_rg = n_tok // (NA_QROWS * GRID_W)
    grid_rows = n_tok // GRID_W
    meta0 = n_tok + PAD
    scale = NA_HEAD_DIM ** -0.5
    hw = NA_HG * NA_HEAD_DIM
    km = k_ref[0, meta0:meta0 + N_META, :]
    vm = v_ref[0, meta0:meta0 + N_META, :]

    def masked_queries(q, nq):
        row_head = lax.broadcasted_iota(jnp.int32, (NA_HG * nq, hw), 0) // nq
        lane_head = lax.broadcasted_iota(jnp.int32, (NA_HG * nq, hw), 1) // NA_HEAD_DIM
        qq = jnp.concatenate([q] * NA_HG, axis=0)
        return jnp.where(row_head == lane_head, qq, jnp.zeros_like(qq))

    @pl.when(rg < n_rg)
    def _grid_rows():
        lane_head = lax.broadcasted_iota(jnp.int32, (GRID_W, hw), 1) // NA_HEAD_DIM

        def row_step(rr, carry):
            r = rg * NA_QROWS + rr
            rs = jnp.clip(r - NA_ROWS // 2, 0, grid_rows - NA_ROWS)
            q0 = pl.multiple_of(rr * GRID_W, GRID_W)
            k0 = pl.multiple_of(rs * GRID_W, GRID_W)
            q = q_ref[0, pl.ds(q0, GRID_W), :] * scale
            qm = masked_queries(q.astype(q_ref.dtype), GRID_W)
            ks = k_ref[0, pl.ds(k0, NA_ROWS * GRID_W), :]
            vs = v_ref[0, pl.ds(k0, NA_ROWS * GRID_W), :]
            s = _dot_nt(qm, ks) + bias_ref[r - rs, 0]
            sm = _dot_nt(qm, km)
            fold = lambda x, op: functools.reduce(op, [x[:, i:i + 128] for i in range(0, x.shape[1], 128)])
            sm_wide = jnp.concatenate([sm, jnp.full((sm.shape[0], 128 - N_META), NEG, F32)], axis=1)
            mx = jnp.max(jnp.maximum(fold(s, jnp.maximum), sm_wide), axis=-1, keepdims=True)
            p = jnp.exp(s - mx)
            pm = jnp.exp(sm_wide - mx)
            den = jnp.sum(fold(p, jnp.add) + pm, axis=-1, keepdims=True)
            o = (_dot(_mm(p), vs) + _dot(_mm(pm[:, 0:N_META]), vm)) * (1.0 / den)
            o_ref[0, pl.ds(q0, GRID_W), :] = _head_select(o, GRID_W, lane_head).astype(o_ref.dtype)
            return carry

        lax.fori_loop(0, NA_QROWS, row_step, 0, unroll=True)

    @pl.when(rg == n_rg)
    def _meta_rows():
        lane_head = lax.broadcasted_iota(jnp.int32, (N_META, hw), 1) // NA_HEAD_DIM
        q = q_ref[0, PAD:PAD + N_META, :] * scale
        qm = masked_queries(q.astype(q_ref.dtype), N_META)
        s = _dot_nt(qm, km)
        p = jnp.exp(s - jnp.max(s, axis=-1, keepdims=True))
        o = _dot(_mm(p), vm) / jnp.sum(p, axis=-1, keepdims=True)
        o_ref[0, 0:PAD, :] = jnp.zeros((PAD, hw), o_ref.dtype)
        o_ref[0, PAD:BLK, :] = _head_select(o, N_META, lane_head).astype(o_ref.dtype)


def _neighborhood_attention(qkv, bias, bsz, n_tok):
    l_pad = n_tok + BLK
    hw = NA_HG * NA_HEAD_DIM
    n_hg = NA_HEADS // NA_HG
    qrows = NA_QROWS * GRID_W
    whole = lambda off: pl.BlockSpec((1, l_pad, hw), lambda b, g, r: (b, 0, off + g), pipeline_mode=pl.Buffered(1))
    return pl.pallas_call(
        functools.partial(_na_body, n_tok=n_tok),
        grid=(bsz, n_hg, n_tok // qrows + 1),
        in_specs=[
            pl.BlockSpec((1, qrows, hw), lambda b, g, r: (b, r, g)),
            whole(n_hg),
            whole(2 * n_hg),
            pl.BlockSpec((NA_ROWS, 1, NA_HG * GRID_W, NA_ROWS * GRID_W), lambda b, g, r: (0, g, 0, 0)),
        ],
        out_specs=pl.BlockSpec((1, qrows, hw), lambda b, g, r: (b, r, g)),
        out_shape=jax.ShapeDtypeStruct((bsz, l_pad, NA_WIDTH), MM_DTYPE),
        compiler_params=_params(("parallel", "parallel", "arbitrary")),
        name="neighborhood_attention",
    )(qkv, qkv, qkv, bias)


def _hgrn_masks(rev):
    C, c = BLK, HG_SUB
    i = np.arange(C)
    s = np.arange(C // c)[:, None]
    d = np.arange(c)[:, None]
    keys_ok = (i[None] >= (s + 1) * c) if rev else (i[None] < s * c)
    diag_ok = (i[None] % c + d <= c - 1) if rev else (i[None] % c >= d)
    table = lambda ok: np.broadcast_to(np.where(ok, 0.0, NEG).astype(np.float32)[:, :, None], ok.shape + (HG_WIDTH,))
    return jnp.asarray(table(keys_ok)), jnp.asarray(table(diag_ok))


def _cumsum_rows(tri, x):
    hi = _mm(x)
    rest = x - hi.astype(F32)
    mid = _mm(rest)
    lo = _mm(rest - mid.astype(F32))
    w = x.shape[1]
    y = _dot(tri, jnp.concatenate([hi, mid, lo], axis=1))
    return y[:, 0:w] + y[:, w:2 * w] + y[:, 2 * w:3 * w]


def _hgrn_chunk(q, g2, lk, v, keys_ref, diag_ref, head_mask, hsum, st_ref, rev):
    C, c = BLK, HG_SUB
    nsub = C // c
    w = HG_WIDTH
    ri = lax.broadcasted_iota(jnp.int32, (C, C), 0)
    ci = lax.broadcasted_iota(jnp.int32, (C, C), 1)
    tri = jnp.where((ci >= ri) if rev else (ci <= ri), 1.0, 0.0).astype(MM_DTYPE)
    b2 = _cumsum_rows(tri, g2)
    kb = b2 - lk
    edge = b2[0:1] if rev else b2[C - 1:C]
    st = st_ref[...]
    acc = _dot_nt(_mm(q * jnp.exp2(b2)), _mm(st))
    d_t = _dot_tn(_mm(v), _mm(jnp.exp2(edge - kb)))
    st_ref[...] = st * jnp.exp2(edge) + d_t * head_mask
    vmm = _mm(v)
    row_head = lax.broadcasted_iota(jnp.int32, (HG_HEADS * c, w), 0) // c
    lane_head_q = lax.broadcasted_iota(jnp.int32, (HG_HEADS * c, w), 1) // HG_DK
    lane_head = lax.broadcasted_iota(jnp.int32, (c, w), 1) // HG_DK
    subs = range(nsub - 1) if rev else range(1, nsub)
    atts = []
    for s in subs:
        lo, hi = c * s, c * (s + 1)
        beta = b2[hi:hi + 1] if rev else b2[lo - 1:lo]
        qt = q[lo:hi] * jnp.exp2(b2[lo:hi] - beta)
        kt = jnp.exp2((beta - kb) + keys_ref[s])
        qh = jnp.where(row_head == lane_head_q, jnp.concatenate([qt] * HG_HEADS, axis=0), 0.0)
        atts.append(_dot_nt(_mm(qh), _mm(kt)))
    r = _dot(_mm(jnp.concatenate(atts, axis=0)), vmm)
    zero = jnp.zeros((c, w), F32)
    outs = []
    for n, s in enumerate(subs):
        o = zero
        for h in range(HG_HEADS):
            row = (n * HG_HEADS + h) * c
            o = o + jnp.where(lane_head == h, r[row:row + c], 0.0)
        outs.append(o)
    acc = acc + jnp.concatenate(outs + [zero] if rev else [zero] + outs, axis=0)
    ps = [_mm(q * jnp.exp2(lk))]
    for d in range(1, c):
        shift = C - d if rev else d
        ps.append(_mm(q * jnp.exp2((b2 - pltpu.roll(kb, shift, 0)) + diag_ref[d])))
    a = _dot(jnp.concatenate(ps, axis=0), hsum)
    acc = acc + a[0:C] * v
    for d in range(1, c):
        acc = acc + a[d * C:(d + 1) * C] * pltpu.roll(v, C - d if rev else d, 0)
    return acc


def _hgrn_body(qf_ref, gf_ref, lf_ref, vf_ref, qb_ref, gb_ref, lbk_ref, vb_ref, kf_ref, df_ref, kb_ref, db_ref,
               hm_ref, hs_ref, of_ref, ob_ref, stf_ref, stb_ref):
    @pl.when(pl.program_id(1) == 0)
    def _reset():
        stf_ref[...] = jnp.zeros_like(stf_ref)
        stb_ref[...] = jnp.zeros_like(stb_ref)

    consts = (hm_ref[...], hs_ref[...])
    for i in range(qf_ref.shape[0]):
        of_ref[i] = _hgrn_chunk(qf_ref[i], gf_ref[i], lf_ref[i], vf_ref[i], kf_ref, df_ref, *consts, stf_ref.at[i], False)
        ob_ref[i] = _hgrn_chunk(qb_ref[i], gb_ref[i], lbk_ref[i], vb_ref[i], kb_ref, db_ref, *consts, stb_ref.at[i], True)


def _hgrn_scan(hg, bsz, n_tok):
    l_pad = n_tok + BLK
    nb = l_pad // BLK
    m = nb - 1
    fwd = lambda n: (n + m) % nb
    bwd = lambda n: jnp.where(n == m, m, m - 1 - n)
    hb = max(d for d in range(1, HG_BATCH + 1) if bsz % d == 0)
    spec = lambda order, col: pl.BlockSpec((hb, BLK, HG_WIDTH), lambda b, n: (b, order(n), col))
    masks = _hgrn_masks(False) + _hgrn_masks(True) + (_head_block_mask(F32), _head_block_mask(MM_DTYPE))
    return pl.pallas_call(
        _hgrn_body,
        grid=(bsz // hb, nb),
        in_specs=[spec(fwd, 0), spec(fwd, 1), spec(fwd, 5), spec(fwd, 3),
                  spec(bwd, 0), spec(bwd, 2), spec(bwd, 6), spec(bwd, 3)] + [_const_spec(t.shape) for t in masks],
        out_specs=[spec(fwd, 0), spec(bwd, 0)],
        out_shape=[jax.ShapeDtypeStruct((bsz, l_pad, HG_WIDTH), F32)] * 2,
        scratch_shapes=[pltpu.VMEM((hb, HG_WIDTH, HG_WIDTH), F32)] * 2,
        compiler_params=_params(("parallel", "arbitrary")),
        name="hgrn2_scan",
    )(hg, hg, hg, hg, hg, hg, hg, hg, *masks)


def _out_ffn_body(h_ref, ys_ref, ua_ref, att_ref, of_ref, ob_ref, gc_ref, gt_ref,
                  d_ref, wglu_ref, on_ref, hmean_ref, wa_ref, wb_ref, wc_ref, wo_ref, n2_ref, wg_ref, wu_ref, wd_ref,
                  out_ref, *, blocks_per_seq):
    tm = h_ref.shape[0]
    y = jax.nn.gelu(ys_ref[...] + d_ref[...] * ua_ref[...])
    y = y * _sigmoid(_dot(_mm(y), wglu_ref[...]))
    o = of_ref[...] + ob_ref[...]
    sq = o * o
    sq_hi = _mm(sq)
    ms = _dot(sq_hi, hmean_ref[...]) + _dot(_mm(sq - sq_hi.astype(F32)), hmean_ref[...])
    o = o * lax.rsqrt(ms + EPS) * on_ref[...]
    gc = gc_ref[...]
    o = o * (gc * _sigmoid(gc))
    mix = (_sigmoid(gt_ref[:, 0:D_MODEL]) * _dot(_mm(y), wa_ref[...])
           + _sigmoid(gt_ref[:, D_MODEL:2 * D_MODEL]) * _dot(att_ref[...], wb_ref[...])
           + _sigmoid(gt_ref[:, 2 * D_MODEL:3 * D_MODEL]) * _dot(_mm(o), wc_ref[...]))
    h = h_ref[...] + _dot(_mm(mix), wo_ref[...])
    hn = _mm(h * lax.rsqrt(jnp.mean(h * h, axis=-1, keepdims=True) + EPS) * n2_ref[...])
    for c0 in range(0, FFN_HIDDEN, FFN_CHUNK):
        gate = _dot(hn, wg_ref[:, c0:c0 + FFN_CHUNK])
        up = _dot(hn, wu_ref[:, c0:c0 + FFN_CHUNK])
        h = h + _dot(_mm(gate * _sigmoid(gate) * up), wd_ref[c0:c0 + FFN_CHUNK, :])
    out_ref[...] = h
    first_blk = pl.program_id(0) * (tm // BLK)
    for j in range(tm // BLK):
        @pl.when((first_blk + j) % blocks_per_seq == blocks_per_seq - 1)
        def _zero_rows():
            out_ref[j * BLK:j * BLK + PAD, :] = jnp.zeros((PAD, D_MODEL), F32)


def _out_ffn(h2d, ys, ua, att, o_f, o_b, hg, gates, lw, blocks_per_seq):
    rows = h2d.shape[0]
    tm = ROW_TILE_OUT
    row = lambda width, col=0: pl.BlockSpec((tm, width), lambda i: (i, col))
    weights = [lw["d"], lw["w_glu"], lw["onorm"], _head_block_mask(MM_DTYPE, 1.0 / HG_DK),
               lw["w_up_a"], lw["w_up_b"], lw["w_up_c"], lw["w_o"],
               lw["norm2"], lw["w_ffn_gate"], lw["w_ffn_up"], lw["w_ffn_down"]]
    return pl.pallas_call(
        functools.partial(_out_ffn_body, blocks_per_seq=blocks_per_seq),
        grid=(pl.cdiv(rows, tm),),
        in_specs=[row(D_MODEL), row(S5_WIDTH), row(S5_WIDTH), row(NA_WIDTH), row(HG_WIDTH), row(HG_WIDTH),
                  row(HG_WIDTH, 4), row(GATE_W)] + [_const_spec(w.shape) for w in weights],
        out_specs=row(D_MODEL),
        out_shape=jax.ShapeDtypeStruct((rows, D_MODEL), F32),
        compiler_params=_params(("parallel",)),
        name="mix_out_ffn",
    )(h2d, ys, ua, att, o_f, o_b, hg, gates, *weights)


def _final_norm_body(h_ref, g_ref, o_ref):
    x = h_ref[0]
    o_ref[0] = x * lax.rsqrt(jnp.mean(x * x, axis=-1, keepdims=True) + EPS) * g_ref[...]


def _final_norm(h, g, n_tok):
    bsz = h.shape[0]
    tl = 512
    spec = pl.BlockSpec((1, tl, D_MODEL), lambda b, j: (b, j, 0))
    return pl.pallas_call(
        _final_norm_body,
        grid=(bsz, n_tok // tl),
        in_specs=[spec, pl.BlockSpec((1, D_MODEL), lambda b, j: (0, 0))],
        out_specs=spec,
        out_shape=jax.ShapeDtypeStruct((bsz, n_tok, D_MODEL), F32),
        compiler_params=_params(("parallel", "parallel")),
        name="final_norm",
    )(h, g)


def _trunk(x, meta_tokens, layers, final_g):
    bsz, n_tok, _ = x.shape
    l_pad = n_tok + BLK
    rows = bsz * l_pad
    meta = jnp.broadcast_to(meta_tokens[None].astype(F32), (bsz, N_META, D_MODEL))
    h = jnp.concatenate([x.astype(F32), jnp.zeros((bsz, PAD, D_MODEL), F32), meta], axis=1).reshape(rows, D_MODEL)
    for lw in layers:
        ua, qkv, hg, gates = _in_proj(h, lw["norm1"], lw["w_in"], lw["lb"])
        ys = _s5_scan(ua, lw["s5_ops"], bsz, n_tok)
        att = _neighborhood_attention(qkv.reshape(bsz, l_pad, QKV_W), lw["na_bias"], bsz, n_tok)
        o_f, o_b = _hgrn_scan(hg.reshape(bsz, l_pad, HG_IN_W), bsz, n_tok)
        h = _out_ffn(h, ys, ua, att.reshape(rows, NA_WIDTH), o_f.reshape(rows, HG_WIDTH),
                     o_b.reshape(rows, HG_WIDTH), hg, gates, lw, l_pad // BLK)
    return _final_norm(h.reshape(bsz, l_pad, D_MODEL), final_g, n_tok)


def kernel(x_prompt, x_sample, meta_tokens, norm1_g, w_in, s5_a_re, s5_a_im, s5_log_dt, s5_b_re, s5_b_im,
           s5_c_re, s5_c_im, s5_d, s5_w_glu, na_rpb, hg_lb_logits, hg_onorm_g, w_up_a, w_up_b, w_up_c,
           w_o, norm2_g, w_ffn_gate, w_ffn_up, w_ffn_down, final_norm_g):
    depth = w_in.shape[0]
    sm = jax.nn.softmax(hg_lb_logits.astype(F32), axis=0)
    lbs = jnp.cumsum(sm, axis=0) - sm[0:1]
    layers = []
    for l in range(depth):
        layers.append(dict(
            norm1=norm1_g[l].astype(F32)[None],
            w_in=_mm(w_in[l]),
            s5_ops=_s5_operators(s5_a_re[l].astype(F32), s5_a_im[l].astype(F32), s5_log_dt[l].astype(F32),
                                 s5_b_re[l].astype(F32), s5_b_im[l].astype(F32), s5_c_re[l].astype(F32),
                                 s5_c_im[l].astype(F32)),
            na_bias=_na_bias(na_rpb[l]),
            lb=lbs[l][None],
            d=s5_d[l].astype(F32)[None],
            w_glu=_mm(s5_w_glu[l]),
            onorm=jnp.tile(hg_onorm_g[l].astype(F32), HG_HEADS)[None],
            w_up_a=_mm(w_up_a[l]), w_up_b=_mm(w_up_b[l]), w_up_c=_mm(w_up_c[l]), w_o=_mm(w_o[l]),
            norm2=norm2_g[l].astype(F32)[None],
            w_ffn_gate=_mm(w_ffn_gate[l]), w_ffn_up=_mm(w_ffn_up[l]), w_ffn_down=_mm(w_ffn_down[l]),
        ))
    final_g = final_norm_g.astype(F32)[None]
    return (_trunk(x_prompt, meta_tokens, layers, final_g), _trunk(x_sample, meta_tokens, layers, final_g))
```

```python
import functools

import numpy as np
import jax
import jax.numpy as jnp
from jax import lax
from jax.experimental import pallas as pl
from jax.experimental.pallas import tpu as pltpu

F32 = jnp.float32
MM_DTYPE = jnp.bfloat16

D_MODEL = 1024
N_META = 16
GRID_W = 64
EPS = 1e-6
NEG = -1e30
BLK = 64
PAD = BLK - N_META
S5_WIDTH = 256
S5_GROUP = 16
S5_GROUPS = 16
S5_STATE = 64
S5_T = 64
S5_PACK_CHUNKS = 16
NA_HEADS = 8
NA_HEAD_DIM = 64
NA_WIDTH = 512
NA_ROWS = 8
NA_COLS = 16
NA_HG = 4
NA_QROWS = 16
HG_HEADS = 4
HG_DK = 64
HG_WIDTH = 256
HG_SUB = 8
HG_BATCH = 4
FFN_HIDDEN = 2816
FFN_CHUNK = 2816
IN_WIDTH = 6144
QKV_W = 3 * NA_WIDTH
HG_IN_W = 7 * HG_WIDTH
GATE_W = 3 * D_MODEL
ROW_TILE_IN = 256
ROW_TILE_OUT = 256
VMEM_LIMIT = 56 * 1024 * 1024


def _dot(a, b):
    return jnp.dot(a, b, preferred_element_type=F32)


def _dot_nt(a, b):
    return lax.dot_general(a, b, (((1,), (1,)), ((), ())), preferred_element_type=F32)


def _dot_tn(a, b):
    return lax.dot_general(a, b, (((0,), (0,)), ((), ())), preferred_element_type=F32)


def _dot_f32(a, b):
    return jnp.dot(a, b, preferred_element_type=F32, precision=lax.Precision.HIGHEST)


def _mm(x):
    return x.astype(MM_DTYPE)


def _sigmoid(x):
    return 0.5 * jnp.tanh(0.5 * x) + 0.5


def _head_block_mask(dtype, scale=1.0):
    r = np.arange(HG_WIDTH) // HG_DK
    return jnp.asarray(np.where(r[:, None] == r[None, :], scale, 0.0), dtype)


def _const_spec(shape):
    nd = len(shape)
    return pl.BlockSpec(shape, lambda *_: (0,) * nd, pipeline_mode=pl.Buffered(1))


def _params(sem):
    return pltpu.CompilerParams(dimension_semantics=sem, vmem_limit_bytes=VMEM_LIMIT)


def _in_proj_body(h_ref, g_ref, w_ref, lb_ref, ua_ref, qkv_ref, hg_ref, gt_ref):
    x = h_ref[...]
    xn = x * lax.rsqrt(jnp.mean(x * x, axis=-1, keepdims=True) + EPS) * g_ref[...]
    xn = _mm(xn)
    w_hg = HG_WIDTH
    c_qkv = S5_WIDTH
    c_hg = c_qkv + QKV_W
    c_gt = c_hg + 5 * w_hg
    z = _dot(xn, w_ref[:, c_hg:c_gt])
    q = z[:, 0:w_hg]
    hg_ref[:, 0:w_hg] = q * _sigmoid(q)
    hg_ref[:, 3 * w_hg:5 * w_hg] = z[:, 3 * w_hg:5 * w_hg]
    lb = lb_ref[...]
    for src, dst in ((1, 5), (2, 6)):
        sig = jax.nn.sigmoid(z[:, src * w_hg:(src + 1) * w_hg])
        hg_ref[:, src * w_hg:(src + 1) * w_hg] = jnp.log2(lb + (1.0 - lb) * sig)
        hg_ref[:, dst * w_hg:(dst + 1) * w_hg] = jnp.log2((1.0 - lb) * (1.0 - sig))
    ua_ref[...] = _dot(xn, w_ref[:, 0:c_qkv])
    qkv_ref[...] = _dot(xn, w_ref[:, c_qkv:c_hg]).astype(qkv_ref.dtype)
    gt_ref[...] = _dot(xn, w_ref[:, c_gt:c_gt + GATE_W])


def _in_proj(h2d, g, w, lb):
    rows = h2d.shape[0]
    tm = ROW_TILE_IN
    row = lambda width: pl.BlockSpec((tm, width), lambda i: (i, 0))
    return pl.pallas_call(
        _in_proj_body,
        grid=(pl.cdiv(rows, tm),),
        in_specs=[row(D_MODEL), _const_spec((1, D_MODEL)), _const_spec((D_MODEL, IN_WIDTH)), _const_spec((1, HG_WIDTH))],
        out_specs=[row(S5_WIDTH), row(QKV_W), row(HG_IN_W), row(GATE_W)],
        out_shape=[
            jax.ShapeDtypeStruct((rows, S5_WIDTH), F32),
            jax.ShapeDtypeStruct((rows, QKV_W), MM_DTYPE),
            jax.ShapeDtypeStruct((rows, HG_IN_W), F32),
            jax.ShapeDtypeStruct((rows, GATE_W), F32),
        ],
        compiler_params=_params(("parallel",)),
        name="in_proj",
    )(h2d, g, w, lb)


def _s5_operators(a_re, a_im, log_dt, b_re, b_im, c_re, c_im):
    T, G, P, C = S5_T, S5_GROUPS, S5_STATE, S5_GROUP
    hi = lax.Precision.HIGHEST
    dt = jnp.exp(log_dt)[..., None]
    mag = jnp.exp(a_re * dt)
    lr = mag * jnp.cos(a_im * dt)
    li = mag * jnp.sin(a_im * dt)
    den = a_re * a_re + a_im * a_im
    nr = lr - 1.0
    z_re = (nr * a_re + li * a_im) / den
    z_im = (li * a_re - nr * a_im) / den
    bb_re = z_re[..., None] * b_re - z_im[..., None] * b_im
    bb_im = z_re[..., None] * b_im + z_im[..., None] * b_re
    d = jnp.arange(T + 1, dtype=F32)
    pmag = jnp.exp((a_re * dt)[..., None] * d)
    pr = pmag * jnp.cos((a_im * dt)[..., None] * d)
    pi = pmag * jnp.sin((a_im * dt)[..., None] * d)
    lb_re = pr[..., None] * bb_re[:, :, :, None, :] - pi[..., None] * bb_im[:, :, :, None, :]
    lb_im = pr[..., None] * bb_im[:, :, :, None, :] + pi[..., None] * bb_re[:, :, :, None, :]
    cl_re = c_re[..., None] * pr[:, :, None] - c_im[..., None] * pi[:, :, None]
    cl_im = c_re[..., None] * pi[:, :, None] + c_im[..., None] * pr[:, :, None]
    kk = (jnp.einsum('xgop,xgpdi->xgdoi', c_re, lb_re, precision=hi)
          - jnp.einsum('xgop,xgpdi->xgdoi', c_im, lb_im, precision=hi))[:, :, :T]
    kf, kb = kk[0], kk[1]
    kall = jnp.concatenate([kb[:, :0:-1], (kf[:, :1] + kb[:, :1]), kf[:, 1:]], axis=1)
    kall_t = kall.transpose(0, 3, 1, 2).reshape(G, C, (2 * T - 1) * C)

    def pack_cols(re, im):
        z = jnp.zeros_like(re)
        odd = (jnp.arange(G) % 2 == 1)[:, None, None]
        return jnp.concatenate([jnp.where(odd, z, re), jnp.where(odd, re, z),
                                jnp.where(odd, z, im), jnp.where(odd, im, z)], axis=-1)

    vf_re = lb_re[0][:, :, T - 1::-1][:, :, :T].transpose(0, 2, 3, 1).reshape(G, T * C, P)
    vf_im = lb_im[0][:, :, T - 1::-1][:, :, :T].transpose(0, 2, 3, 1).reshape(G, T * C, P)
    vb_re = lb_re[1][:, :, :T].transpose(0, 2, 3, 1).reshape(G, T * C, P)
    vb_im = lb_im[1][:, :, :T].transpose(0, 2, 3, 1).reshape(G, T * C, P)
    v_f = pack_cols(vf_re, vf_im)
    v_b = pack_cols(vb_re, vb_im)
    wf_re = cl_re[0][..., 1:T + 1].transpose(0, 2, 3, 1).reshape(G, P, T * C)
    wf_im = cl_im[0][..., 1:T + 1].transpose(0, 2, 3, 1).reshape(G, P, T * C)
    wb_re = cl_re[1][..., T:0:-1].transpose(0, 2, 3, 1).reshape(G, P, T * C)
    wb_im = cl_im[1][..., T:0:-1].transpose(0, 2, 3, 1).reshape(G, P, T * C)
    pack_rows = lambda re, im: pack_cols(re.transpose(0, 2, 1), -im.transpose(0, 2, 1)).transpose(0, 2, 1)
    w_f = pack_rows(wf_re, wf_im)
    w_b = pack_rows(wb_re, wb_im)
    lam = jnp.stack([pr[0, :, :, T], pi[0, :, :, T], pr[1, :, :, T], pi[1, :, :, T]], axis=0)
    lam = lam.reshape(4, G // 2, 2 * P).transpose(1, 0, 2)
    return kall_t, _mm(v_f), _mm(v_b), _mm(w_f), _mm(w_b), lam


def _s5_pack_body(u_ref, o_ref):
    nbb = o_ref.shape[1]
    per_vreg = 128 // S5_GROUP
    for j in range(S5_T // per_vreg):
        slabs = [u_ref[pl.ds(j * per_vreg + t, nbb, stride=S5_T), :] for t in range(per_vreg)]
        for g in range(per_vreg):
            blk = jnp.concatenate([s[:, g * S5_GROUP:(g + 1) * S5_GROUP] for s in slabs], axis=1)
            o_ref[g, :, j * 128:(j + 1) * 128] = blk.astype(o_ref.dtype)


def _s5_unpack_body(y_ref, o_ref):
    nbb = y_ref.shape[1]
    per_vreg = 128 // S5_GROUP
    for j in range(S5_T // per_vreg):
        blks = [y_ref[g, :, j * 128:(j + 1) * 128] for g in range(per_vreg)]
        for t in range(per_vreg):
            slab = jnp.concatenate([b[:, t * S5_GROUP:(t + 1) * S5_GROUP] for b in blks], axis=1)
            o_ref[pl.ds(j * per_vreg + t, nbb, stride=S5_T), :] = slab


def _s5_inc_body(u_ref, vf_ref, vb_ref, incf_ref, incb_ref):
    incf_ref[...] = _dot(u_ref[0], vf_ref[0]) + _dot(u_ref[1], vf_ref[1])
    incb_ref[...] = _dot(u_ref[0], vb_ref[0]) + _dot(u_ref[1], vb_ref[1])


def _s5_scan_body(ifr_ref, ifi_ref, ibr_ref, ibi_ref, lam_ref, xfr_ref, xfi_ref, xbr_ref, xbi_ref, *, bsz, nc):
    lam = lam_ref[0]
    zero = jnp.zeros((bsz, 2 * S5_STATE), F32)
    m = nc - 1

    def run(inc_re, inc_im, x_re, x_im, lr, li, reverse):
        def step(i, carry):
            if reverse:
                j = jnp.where(i == m, m, m - 1 - i)
            else:
                j = jnp.where(i == 0, m, i - 1)
            rows = pl.ds(j, bsz, stride=nc)
            xr, xi = carry
            x_re[rows, :] = xr
            x_im[rows, :] = xi
            return (lr * xr - li * xi + inc_re[rows, :], lr * xi + li * xr + inc_im[rows, :])

        lax.fori_loop(0, nc, step, (zero, zero))

    run(ifr_ref, ifi_ref, xfr_ref, xfi_ref, lam[0:1], lam[1:2], False)
    run(ibr_ref, ibi_ref, xbr_ref, xbi_ref, lam[2:3], lam[3:4], True)


def _s5_out_body(u_ref, kall_ref, wf_ref, wb_ref, xfr_ref, xfi_ref, xbr_ref, xbi_ref, y_ref, m_ref):
    xf = _mm(jnp.concatenate([xfr_ref[...], xfi_ref[...]], axis=1))
    xb = _mm(jnp.concatenate([xbr_ref[...], xbi_ref[...]], axis=1))
    tw = S5_T * S5_GROUP
    for k in range(2):
        kall = kall_ref[k]
        for s in range(S5_T):
            off = (S5_T - 1 - s) * S5_GROUP
            m_ref[k, s * S5_GROUP:(s + 1) * S5_GROUP, :] = kall[:, off:off + tw].astype(m_ref.dtype)
        y_ref[k] = _dot(u_ref[k], m_ref[k]) + _dot(xf, wf_ref[k]) + _dot(xb, wb_ref[k])


def _s5_scan(u_a, ops, layer, bsz, n_tok):
    kall_t, v_f, v_b, w_f, w_b, lam = ops
    T, G, C = S5_T, S5_GROUPS, S5_GROUP
    l_pad = n_tok + BLK
    nc = l_pad // T
    rows = nc * bsz
    tw = T * C
    sw = 4 * S5_STATE
    nbb = S5_PACK_CHUNKS
    gh = 128 // C
    tok_half = pl.BlockSpec((nbb * T, 128), lambda i, h: (i, h))
    grp_half = pl.BlockSpec((gh, nbb, tw), lambda i, h: (h, i, 0))
    u = pl.pallas_call(
        _s5_pack_body,
        grid=(pl.cdiv(rows, nbb), G // gh),
        in_specs=[tok_half],
        out_specs=grp_half,
        out_shape=jax.ShapeDtypeStruct((G, rows, tw), MM_DTYPE),
        compiler_params=_params(("parallel", "parallel")),
        name="s5_pack",
    )(u_a)
    pair3 = lambda r, c: pl.BlockSpec((2, r, c), lambda p: (p, 0, 0))
    op3 = lambda r, c: pl.BlockSpec((None, 2, r, c), lambda p: (layer, p, 0, 0))
    lane = lambda r: pl.BlockSpec((r, sw), lambda p: (0, p))
    re = pl.BlockSpec((rows, sw // 2), lambda p: (0, 2 * p))
    im = pl.BlockSpec((rows, sw // 2), lambda p: (0, 2 * p + 1))
    half = pl.BlockSpec((rows, sw // 2), lambda p: (0, p))
    inc_f, inc_b = pl.pallas_call(
        _s5_inc_body,
        grid=(G // 2,),
        in_specs=[pair3(rows, tw), op3(tw, sw), op3(tw, sw)],
        out_specs=[lane(rows), lane(rows)],
        out_shape=[jax.ShapeDtypeStruct((rows, sw * G // 2), F32)] * 2,
        compiler_params=_params(("parallel",)),
        name="s5_inc",
    )(u, v_f, v_b)
    states = pl.pallas_call(
        functools.partial(_s5_scan_body, bsz=bsz, nc=nc),
        grid=(G // 2,),
        in_specs=[re, im, re, im, pl.BlockSpec((None, 1, 4, 2 * S5_STATE), lambda p: (layer, p, 0, 0))],
        out_specs=[half] * 4,
        out_shape=[jax.ShapeDtypeStruct((rows, sw * G // 4), F32)] * 4,
        compiler_params=_params(("parallel",)),
        name="s5_chunk_scan",
    )(inc_f, inc_f, inc_b, inc_b, lam)
    y = pl.pallas_call(
        _s5_out_body,
        grid=(G // 2,),
        in_specs=[pair3(rows, tw), op3(C, (2 * T - 1) * C), op3(sw, tw), op3(sw, tw)] + [half] * 4,
        out_specs=pair3(rows, tw),
        out_shape=jax.ShapeDtypeStruct((G, rows, tw), F32),
        scratch_shapes=[pltpu.VMEM((2, tw, tw), MM_DTYPE)],
        compiler_params=_params(("parallel",)),
        name="s5_out",
    )(u, kall_t, w_f, w_b, *states)
    return pl.pallas_call(
        _s5_unpack_body,
        grid=(pl.cdiv(rows, nbb), G // gh),
        in_specs=[grp_half],
        out_specs=tok_half,
        out_shape=jax.ShapeDtypeStruct((rows * T, S5_WIDTH), F32),
        compiler_params=_params(("parallel", "parallel")),
        name="s5_unpack",
    )(y)


def _na_bias(rpb):
    cols = np.arange(GRID_W)
    wstart = np.clip(cols - NA_COLS // 2, 0, GRID_W - NA_COLS)
    inwin = (cols[None, :] >= wstart[:, None]) & (cols[None, :] < wstart[:, None] + NA_COLS)
    dcol = np.clip(cols[None, :] - cols[:, None], -(NA_COLS - 1), NA_COLS - 1) + NA_COLS - 1
    drow = np.arange(NA_ROWS)[None, :] - np.arange(NA_ROWS)[:, None] + NA_ROWS - 1
    b = rpb.astype(F32)[:, drow][:, :, :, dcol]
    b = jnp.where(inwin[None, None, None], b, NEG)
    b = b.transpose(1, 0, 3, 2, 4)
    return b.reshape(NA_ROWS, NA_HEADS // NA_HG, NA_HG * GRID_W, NA_ROWS * GRID_W)


def _head_select(o, rows_per_head, lane_head):
    out = jnp.where(lane_head == 0, o[0:rows_per_head], 0.0)
    for h in range(1, NA_HG):
        out = out + jnp.where(lane_head == h, o[h * rows_per_head:(h + 1) * rows_per_head], 0.0)
    return out


def _na_body(q_ref, k_ref, v_ref, bias_ref, o_ref, *, n_tok):
    rg = pl.program_id(2)
    n_rg = n_tok // (NA_QROWS * GRID_W)
    grid_rows = n_tok // GRID_W
    meta0 = n_tok + PAD
    scale = NA_HEAD_DIM ** -0.5
    hw = NA_HG * NA_HEAD_DIM
    km = k_ref[0, meta0:meta0 + N_META, :]
    vm = v_ref[0, meta0:meta0 + N_META, :]

    def masked_queries(q, nq):
        row_head = lax.broadcasted_iota(jnp.int32, (NA_HG * nq, hw), 0) // nq
        lane_head = lax.broadcasted_iota(jnp.int32, (NA_HG * nq, hw), 1) // NA_HEAD_DIM
        qq = jnp.concatenate([q] * NA_HG, axis=0)
        return jnp.where(row_head == lane_head, qq, jnp.zeros_like(qq))

    @pl.when(rg < n_rg)
    def _grid_rows():
        lane_head = lax.broadcasted_iota(jnp.int32, (GRID_W, hw), 1) // NA_HEAD_DIM

        def row_step(rr, carry):
            r = rg * NA_QROWS + rr
            rs = jnp.clip(r - NA_ROWS // 2, 0, grid_rows - NA_ROWS)
            q0 = pl.multiple_of(rr * GRID_W, GRID_W)
            k0 = pl.multiple_of(rs * GRID_W, GRID_W)
            q = q_ref[0, pl.ds(q0, GRID_W), :] * scale
            qm = masked_queries(q.astype(q_ref.dtype), GRID_W)
            ks = k_ref[0, pl.ds(k0, NA_ROWS * GRID_W), :]
            vs = v_ref[0, pl.ds(k0, NA_ROWS * GRID_W), :]
            s = _dot_nt(qm, ks) + bias_ref[r - rs, 0]
            sm = _dot_nt(qm, km)
            fold = lambda x, op: functools.reduce(op, [x[:, i:i + 128] for i in range(0, x.shape[1], 128)])
            sm_wide = jnp.concatenate([sm, jnp.full((sm.shape[0], 128 - N_META), NEG, F32)], axis=1)
            mx = jnp.max(jnp.maximum(fold(s, jnp.maximum), sm_wide), axis=-1, keepdims=True)
            p = jnp.exp(s - mx)
            pm = jnp.exp(sm_wide - mx)
            den = jnp.sum(fold(p, jnp.add) + pm, axis=-1, keepdims=True)
            o = (_dot(_mm(p), vs) + _dot(_mm(pm[:, 0:N_META]), vm)) * (1.0 / den)
            o_ref[0, pl.ds(q0, GRID_W), :] = _head_select(o, GRID_W, lane_head).astype(o_ref.dtype)
            return carry

        lax.fori_loop(0, NA_QROWS, row_step, 0, unroll=True)

    @pl.when(rg == n_rg)
    def _meta_rows():
        lane_head = lax.broadcasted_iota(jnp.int32, (N_META, hw), 1) // NA_HEAD_DIM
        q = q_ref[0, PAD:PAD + N_META, :] * scale
        qm = masked_queries(q.astype(q_ref.dtype), N_META)
        s = _dot_nt(qm, km)
        p = jnp.exp(s - jnp.max(s, axis=-1, keepdims=True))
        o = _dot(_mm(p), vm) / jnp.sum(p, axis=-1, keepdims=True)
        o_ref[0, 0:PAD, :] = jnp.zeros((PAD, hw), o_ref.dtype)
        o_ref[0, PAD:BLK, :] = _head_select(o, N_META, lane_head).astype(o_ref.dtype)


def _neighborhood_attention(qkv, bias, layer, bsz, n_tok):
    l_pad = n_tok + BLK
    hw = NA_HG * NA_HEAD_DIM
    n_hg = NA_HEADS // NA_HG
    qrows = NA_QROWS * GRID_W
    whole = lambda off: pl.BlockSpec((1, l_pad, hw), lambda b, g, r: (b, 0, off + g), pipeline_mode=pl.Buffered(1))
    return pl.pallas_call(
        functools.partial(_na_body, n_tok=n_tok),
        grid=(bsz, n_hg, n_tok // qrows + 1),
        in_specs=[
            pl.BlockSpec((1, qrows, hw), lambda b, g, r: (b, r, g)),
            whole(n_hg),
            whole(2 * n_hg),
            pl.BlockSpec((None, NA_ROWS, 1, NA_HG * GRID_W, NA_ROWS * GRID_W), lambda b, g, r: (layer, 0, g, 0, 0)),
        ],
        out_specs=pl.BlockSpec((1, qrows, hw), lambda b, g, r: (b, r, g)),
        out_shape=jax.ShapeDtypeStruct((bsz, l_pad, NA_WIDTH), MM_DTYPE),
        compiler_params=_params(("parallel", "parallel", "arbitrary")),
        name="neighborhood_attention",
    )(qkv, qkv, qkv, bias)


def _hgrn_masks(rev):
    C, c = BLK, HG_SUB
    i = np.arange(C)
    s = np.arange(C // c)[:, None]
    d = np.arange(c)[:, None]
    keys_ok = (i[None] >= (s + 1) * c) if rev else (i[None] < s * c)
    diag_ok = (i[None] % c + d <= c - 1) if rev else (i[None] % c >= d)
    table = lambda ok: np.broadcast_to(np.where(ok, 0.0, NEG).astype(np.float32)[:, :, None], ok.shape + (HG_WIDTH,))
    return jnp.asarray(table(keys_ok)), jnp.asarray(table(diag_ok))


def _cumsum_rows(tri, x):
    hi = _mm(x)
    rest = x - hi.astype(F32)
    mid = _mm(rest)
    lo = _mm(rest - mid.astype(F32))
    w = x.shape[1]
    y = _dot(tri, jnp.concatenate([hi, mid, lo], axis=1))
    return y[:, 0:w] + y[:, w:2 * w] + y[:, 2 * w:3 * w]


def _hgrn_chunk(q, g2, lk, v, keys_ref, diag_ref, head_mask, hsum, st_ref, rev):
    C, c = BLK, HG_SUB
    nsub = C // c
    w = HG_WIDTH
    ri = lax.broadcasted_iota(jnp.int32, (C, C), 0)
    ci = lax.broadcasted_iota(jnp.int32, (C, C), 1)
    tri = jnp.where((ci >= ri) if rev else (ci <= ri), 1.0, 0.0).astype(MM_DTYPE)
    b2 = _cumsum_rows(tri, g2)
    kb = b2 - lk
    edge = b2[0:1] if rev else b2[C - 1:C]
    st = st_ref[...]
    acc = _dot_nt(_mm(q * jnp.exp2(b2)), _mm(st))
    d_t = _dot_tn(_mm(v), _mm(jnp.exp2(edge - kb)))
    st_ref[...] = st * jnp.exp2(edge) + d_t * head_mask
    vmm = _mm(v)
    row_head = lax.broadcasted_iota(jnp.int32, (HG_HEADS * c, w), 0) // c
    lane_head_q = lax.broadcasted_iota(jnp.int32, (HG_HEADS * c, w), 1) // HG_DK
    lane_head = lax.broadcasted_iota(jnp.int32, (c, w), 1) // HG_DK
    subs = range(nsub - 1) if rev else range(1, nsub)
    atts = []
    for s in subs:
        lo, hi = c * s, c * (s + 1)
        beta = b2[hi:hi + 1] if rev else b2[lo - 1:lo]
        qt = q[lo:hi] * jnp.exp2(b2[lo:hi] - beta)
        kt = jnp.exp2((beta - kb) + keys_ref[s])
        qh = jnp.where(row_head == lane_head_q, jnp.concatenate([qt] * HG_HEADS, axis=0), 0.0)
        atts.append(_dot_nt(_mm(qh), _mm(kt)))
    r = _dot(_mm(jnp.concatenate(atts, axis=0)), vmm)
    zero = jnp.zeros((c, w), F32)
    outs = []
    for n, s in enumerate(subs):
        o = zero
        for h in range(HG_HEADS):
            row = (n * HG_HEADS + h) * c
            o = o + jnp.where(lane_head == h, r[row:row + c], 0.0)
        outs.append(o)
    acc = acc + jnp.concatenate(outs + [zero] if rev else [zero] + outs, axis=0)
    ps = [_mm(q * jnp.exp2(lk))]
    for d in range(1, c):
        shift = C - d if rev else d
        ps.append(_mm(q * jnp.exp2((b2 - pltpu.roll(kb, shift, 0)) + diag_ref[d])))
    a = _dot(jnp.concatenate(ps, axis=0), hsum)
    acc = acc + a[0:C] * v
    for d in range(1, c):
        acc = acc + a[d * C:(d + 1) * C] * pltpu.roll(v, C - d if rev else d, 0)
    return acc


def _hgrn_body(qf_ref, gf_ref, lf_ref, vf_ref, qb_ref, gb_ref, lbk_ref, vb_ref, kf_ref, df_ref, kb_ref, db_ref,
               hm_ref, hs_ref, of_ref, ob_ref, stf_ref, stb_ref):
    @pl.when(pl.program_id(1) == 0)
    def _reset():
        stf_ref[...] = jnp.zeros_like(stf_ref)
        stb_ref[...] = jnp.zeros_like(stb_ref)

    consts = (hm_ref[...], hs_ref[...])
    for i in range(qf_ref.shape[0]):
        of_ref[i] = _hgrn_chunk(qf_ref[i], gf_ref[i], lf_ref[i], vf_ref[i], kf_ref, df_ref, *consts, stf_ref.at[i], False)
        ob_ref[i] = _hgrn_chunk(qb_ref[i], gb_ref[i], lbk_ref[i], vb_ref[i], kb_ref, db_ref, *consts, stb_ref.at[i], True)


def _hgrn_scan(hg, bsz, n_tok):
    l_pad = n_tok + BLK
    nb = l_pad // BLK
    m = nb - 1
    fwd = lambda n: (n + m) % nb
    bwd = lambda n: jnp.where(n == m, m, m - 1 - n)
    hb = max(d for d in range(1, HG_BATCH + 1) if bsz % d == 0)
    spec = lambda order, col: pl.BlockSpec((hb, BLK, HG_WIDTH), lambda b, n: (b, order(n), col))
    masks = _hgrn_masks(False) + _hgrn_masks(True) + (_head_block_mask(F32), _head_block_mask(MM_DTYPE))
    return pl.pallas_call(
        _hgrn_body,
        grid=(bsz // hb, nb),
        in_specs=[spec(fwd, 0), spec(fwd, 1), spec(fwd, 5), spec(fwd, 3),
                  spec(bwd, 0), spec(bwd, 2), spec(bwd, 6), spec(bwd, 3)] + [_const_spec(t.shape) for t in masks],
        out_specs=[spec(fwd, 0), spec(bwd, 0)],
        out_shape=[jax.ShapeDtypeStruct((bsz, l_pad, HG_WIDTH), F32)] * 2,
        scratch_shapes=[pltpu.VMEM((hb, HG_WIDTH, HG_WIDTH), F32)] * 2,
        compiler_params=_params(("parallel", "arbitrary")),
        name="hgrn2_scan",
    )(hg, hg, hg, hg, hg, hg, hg, hg, *masks)


def _out_ffn_body(h_ref, ys_ref, ua_ref, att_ref, of_ref, ob_ref, gc_ref, gt_ref,
                  d_ref, wglu_ref, on_ref, hmean_ref, wa_ref, wb_ref, wc_ref, wo_ref, n2_ref, wg_ref, wu_ref, wd_ref,
                  out_ref, *, blocks_per_seq):
    tm = h_ref.shape[0]
    y = jax.nn.gelu(ys_ref[...] + d_ref[...] * ua_ref[...])
    y = y * _sigmoid(_dot(_mm(y), wglu_ref[...]))
    o = of_ref[...] + ob_ref[...]
    sq = o * o
    sq_hi = _mm(sq)
    ms = _dot(sq_hi, hmean_ref[...]) + _dot(_mm(sq - sq_hi.astype(F32)), hmean_ref[...])
    o = o * lax.rsqrt(ms + EPS) * on_ref[...]
    gc = gc_ref[...]
    o = o * (gc * _sigmoid(gc))
    mix = (_sigmoid(gt_ref[:, 0:D_MODEL]) * _dot(_mm(y), wa_ref[...])
           + _sigmoid(gt_ref[:, D_MODEL:2 * D_MODEL]) * _dot(att_ref[...], wb_ref[...])
           + _sigmoid(gt_ref[:, 2 * D_MODEL:3 * D_MODEL]) * _dot(_mm(o), wc_ref[...]))
    h = h_ref[...] + _dot(_mm(mix), wo_ref[...])
    hn = _mm(h * lax.rsqrt(jnp.mean(h * h, axis=-1, keepdims=True) + EPS) * n2_ref[...])
    for c0 in range(0, FFN_HIDDEN, FFN_CHUNK):
        gate = _dot(hn, wg_ref[:, c0:c0 + FFN_CHUNK])
        up = _dot(hn, wu_ref[:, c0:c0 + FFN_CHUNK])
        h = h + _dot(_mm(gate * _sigmoid(gate) * up), wd_ref[c0:c0 + FFN_CHUNK, :])
    out_ref[...] = h
    first_blk = pl.program_id(0) * (tm // BLK)
    for j in range(tm // BLK):
        @pl.when((first_blk + j) % blocks_per_seq == blocks_per_seq - 1)
        def _zero_rows():
            out_ref[j * BLK:j * BLK + PAD, :] = jnp.zeros((PAD, D_MODEL), F32)


def _out_ffn(h2d, ys, ua, att, o_f, o_b, hg, gates, lw, blocks_per_seq):
    rows = h2d.shape[0]
    tm = ROW_TILE_OUT
    row = lambda width, col=0: pl.BlockSpec((tm, width), lambda i: (i, col))
    weights = [lw["d"], lw["w_glu"], lw["onorm"], _head_block_mask(MM_DTYPE, 1.0 / HG_DK),
               lw["w_up_a"], lw["w_up_b"], lw["w_up_c"], lw["w_o"],
               lw["norm2"], lw["w_ffn_gate"], lw["w_ffn_up"], lw["w_ffn_down"]]
    return pl.pallas_call(
        functools.partial(_out_ffn_body, blocks_per_seq=blocks_per_seq),
        grid=(pl.cdiv(rows, tm),),
        in_specs=[row(D_MODEL), row(S5_WIDTH), row(S5_WIDTH), row(NA_WIDTH), row(HG_WIDTH), row(HG_WIDTH),
                  row(HG_WIDTH, 4), row(GATE_W)] + [_const_spec(w.shape) for w in weights],
        out_specs=row(D_MODEL),
        out_shape=jax.ShapeDtypeStruct((rows, D_MODEL), F32),
        compiler_params=_params(("parallel",)),
        name="mix_out_ffn",
    )(h2d, ys, ua, att, o_f, o_b, hg, gates, *weights)


def _final_norm_body(h_ref, g_ref, o_ref):
    x = h_ref[0]
    o_ref[0] = x * lax.rsqrt(jnp.mean(x * x, axis=-1, keepdims=True) + EPS) * g_ref[...]


def _final_norm(h, g, n_tok):
    bsz = h.shape[0]
    tl = 512
    spec = pl.BlockSpec((1, tl, D_MODEL), lambda b, j: (b, j, 0))
    return pl.pallas_call(
        _final_norm_body,
        grid=(bsz, n_tok // tl),
        in_specs=[spec, pl.BlockSpec((1, D_MODEL), lambda b, j: (0, 0))],
        out_specs=spec,
        out_shape=jax.ShapeDtypeStruct((bsz, n_tok, D_MODEL), F32),
        compiler_params=_params(("parallel", "parallel")),
        name="final_norm",
    )(h, g)


def _trunk(x, meta_tokens, layers, s5_ops, na_bias, final_g):
    bsz, n_tok, _ = x.shape
    l_pad = n_tok + BLK
    rows = bsz * l_pad
    meta = jnp.broadcast_to(meta_tokens[None].astype(F32), (bsz, N_META, D_MODEL))
    h = jnp.concatenate([x.astype(F32), jnp.zeros((bsz, PAD, D_MODEL), F32), meta], axis=1).reshape(rows, D_MODEL)
    for l, lw in enumerate(layers):
        ua, qkv, hg, gates = _in_proj(h, lw["norm1"], lw["w_in"], lw["lb"])
        ys = _s5_scan(ua, s5_ops, l, bsz, n_tok)
        att = _neighborhood_attention(qkv.reshape(bsz, l_pad, QKV_W), na_bias, l, bsz, n_tok)
        o_f, o_b = _hgrn_scan(hg.reshape(bsz, l_pad, HG_IN_W), bsz, n_tok)
        h = _out_ffn(h, ys, ua, att.reshape(rows, NA_WIDTH), o_f.reshape(rows, HG_WIDTH),
                     o_b.reshape(rows, HG_WIDTH), hg, gates, lw, l_pad // BLK)
    return _final_norm(h.reshape(bsz, l_pad, D_MODEL), final_g, n_tok)


def kernel(x_prompt, x_sample, meta_tokens, norm1_g, w_in, s5_a_re, s5_a_im, s5_log_dt, s5_b_re, s5_b_im,
           s5_c_re, s5_c_im, s5_d, s5_w_glu, na_rpb, hg_lb_logits, hg_onorm_g, w_up_a, w_up_b, w_up_c,
           w_o, norm2_g, w_ffn_gate, w_ffn_up, w_ffn_down, final_norm_g):
    depth = w_in.shape[0]
    sm = jax.nn.softmax(hg_lb_logits.astype(F32), axis=0)
    lbs = jnp.cumsum(sm, axis=0) - sm[0:1]
    layers = []
    for l in range(depth):
        layers.append(dict(
            norm1=norm1_g[l].astype(F32)[None],
            w_in=_mm(w_in[l]),
            lb=lbs[l][None],
            d=s5_d[l].astype(F32)[None],
            w_glu=_mm(s5_w_glu[l]),
            onorm=jnp.tile(hg_onorm_g[l].astype(F32), HG_HEADS)[None],
            w_up_a=_mm(w_up_a[l]), w_up_b=_mm(w_up_b[l]), w_up_c=_mm(w_up_c[l]), w_o=_mm(w_o[l]),
            norm2=norm2_g[l].astype(F32)[None],
            w_ffn_gate=_mm(w_ffn_gate[l]), w_ffn_up=_mm(w_ffn_up[l]), w_ffn_down=_mm(w_ffn_down[l]),
        ))
    s5_ops = jax.vmap(_s5_operators)(*(t.astype(F32) for t in (s5_a_re, s5_a_im, s5_log_dt, s5_b_re, s5_b_im,
                                                               s5_c_re, s5_c_im)))
    na_bias = jax.vmap(_na_bias)(na_rpb)
    final_g = final_norm_g.astype(F32)[None]
    return tuple(_trunk(x, meta_tokens, layers, s5_ops, na_bias, final_g) for x in (x_prompt, x_sample))
```

```python
import functools

import numpy as np
import jax
import jax.numpy as jnp
from jax import lax
from jax.experimental import pallas as pl
from jax.experimental.pallas import tpu as pltpu

F32 = jnp.float32
MM_DTYPE = jnp.bfloat16

D_MODEL = 1024
N_META = 16
GRID_W = 64
EPS = 1e-6
NEG = -1e30
BLK = 64
PAD = BLK - N_META
S5_WIDTH = 256
S5_GROUP = 16
S5_GROUPS = 16
S5_STATE = 64
S5_T = 64
S5_PACK_CHUNKS = 16
NA_HEADS = 8
NA_HEAD_DIM = 64
NA_WIDTH = 512
NA_ROWS = 8
NA_COLS = 16
NA_HG = 4
NA_QROWS = 32
HG_HEADS = 4
HG_DK = 64
HG_WIDTH = 256
HG_SUB = 8
HG_BATCH = 8
FFN_HIDDEN = 2816
FFN_CHUNK = 2816
IN_WIDTH = 6144
QKV_W = 3 * NA_WIDTH
HG_IN_W = 7 * HG_WIDTH
GATE_W = 3 * D_MODEL
ROW_TILE_IN = 256
ROW_TILE_OUT = 512
VMEM_LIMIT = 56 * 1024 * 1024


def _dot(a, b):
    return jnp.dot(a, b, preferred_element_type=F32)


def _dot_nt(a, b):
    return lax.dot_general(a, b, (((1,), (1,)), ((), ())), preferred_element_type=F32)


def _dot_tn(a, b):
    return lax.dot_general(a, b, (((0,), (0,)), ((), ())), preferred_element_type=F32)


def _dot_f32(a, b):
    return jnp.dot(a, b, preferred_element_type=F32, precision=lax.Precision.HIGHEST)


def _mm(x):
    return x.astype(MM_DTYPE)


def _sigmoid(x):
    return 0.5 * jnp.tanh(0.5 * x) + 0.5


def _head_block_mask(dtype, scale=1.0):
    r = np.arange(HG_WIDTH) // HG_DK
    return jnp.asarray(np.where(r[:, None] == r[None, :], scale, 0.0), dtype)


def _const_spec(shape):
    nd = len(shape)
    return pl.BlockSpec(shape, lambda *_: (0,) * nd, pipeline_mode=pl.Buffered(1))


def _params(sem):
    return pltpu.CompilerParams(dimension_semantics=sem, vmem_limit_bytes=VMEM_LIMIT)


def _in_proj_body(h_ref, g_ref, w_ref, lb_ref, ua_ref, qkv_ref, hg_ref, gt_ref):
    x = h_ref[...]
    xn = x * lax.rsqrt(jnp.mean(x * x, axis=-1, keepdims=True) + EPS) * g_ref[...]
    xn = _mm(xn)
    w_hg = HG_WIDTH
    c_qkv = S5_WIDTH
    c_hg = c_qkv + QKV_W
    c_gt = c_hg + 5 * w_hg
    z = _dot(xn, w_ref[:, c_hg:c_gt])
    q = z[:, 0:w_hg]
    hg_ref[:, 0:w_hg] = q * _sigmoid(q)
    hg_ref[:, 3 * w_hg:5 * w_hg] = z[:, 3 * w_hg:5 * w_hg]
    lb = lb_ref[...]
    for src, dst in ((1, 5), (2, 6)):
        sig = jax.nn.sigmoid(z[:, src * w_hg:(src + 1) * w_hg])
        hg_ref[:, src * w_hg:(src + 1) * w_hg] = jnp.log2(lb + (1.0 - lb) * sig)
        hg_ref[:, dst * w_hg:(dst + 1) * w_hg] = jnp.log2((1.0 - lb) * (1.0 - sig))
    ua_ref[...] = _dot(xn, w_ref[:, 0:c_qkv])
    qkv_ref[...] = _dot(xn, w_ref[:, c_qkv:c_hg]).astype(qkv_ref.dtype)
    gt_ref[...] = _dot(xn, w_ref[:, c_gt:c_gt + GATE_W])


def _in_proj(h2d, g, w, lb):
    rows = h2d.shape[0]
    tm = ROW_TILE_IN
    row = lambda width: pl.BlockSpec((tm, width), lambda i: (i, 0))
    return pl.pallas_call(
        _in_proj_body,
        grid=(pl.cdiv(rows, tm),),
        in_specs=[row(D_MODEL), _const_spec((1, D_MODEL)), _const_spec((D_MODEL, IN_WIDTH)), _const_spec((1, HG_WIDTH))],
        out_specs=[row(S5_WIDTH), row(QKV_W), row(HG_IN_W), row(GATE_W)],
        out_shape=[
            jax.ShapeDtypeStruct((rows, S5_WIDTH), F32),
            jax.ShapeDtypeStruct((rows, QKV_W), MM_DTYPE),
            jax.ShapeDtypeStruct((rows, HG_IN_W), F32),
            jax.ShapeDtypeStruct((rows, GATE_W), F32),
        ],
        compiler_params=_params(("parallel",)),
        name="in_proj",
    )(h2d, g, w, lb)


def _s5_operators(a_re, a_im, log_dt, b_re, b_im, c_re, c_im):
    T, G, P, C = S5_T, S5_GROUPS, S5_STATE, S5_GROUP
    hi = lax.Precision.HIGHEST
    dt = jnp.exp(log_dt)[..., None]
    mag = jnp.exp(a_re * dt)
    lr = mag * jnp.cos(a_im * dt)
    li = mag * jnp.sin(a_im * dt)
    den = a_re * a_re + a_im * a_im
    nr = lr - 1.0
    z_re = (nr * a_re + li * a_im) / den
    z_im = (li * a_re - nr * a_im) / den
    bb_re = z_re[..., None] * b_re - z_im[..., None] * b_im
    bb_im = z_re[..., None] * b_im + z_im[..., None] * b_re
    d = jnp.arange(T + 1, dtype=F32)
    pmag = jnp.exp((a_re * dt)[..., None] * d)
    pr = pmag * jnp.cos((a_im * dt)[..., None] * d)
    pi = pmag * jnp.sin((a_im * dt)[..., None] * d)
    lb_re = pr[..., None] * bb_re[:, :, :, None, :] - pi[..., None] * bb_im[:, :, :, None, :]
    lb_im = pr[..., None] * bb_im[:, :, :, None, :] + pi[..., None] * bb_re[:, :, :, None, :]
    cl_re = c_re[..., None] * pr[:, :, None] - c_im[..., None] * pi[:, :, None]
    cl_im = c_re[..., None] * pi[:, :, None] + c_im[..., None] * pr[:, :, None]
    kk = (jnp.einsum('xgop,xgpdi->xgdoi', c_re, lb_re, precision=hi)
          - jnp.einsum('xgop,xgpdi->xgdoi', c_im, lb_im, precision=hi))[:, :, :T]
    kf, kb = kk[0], kk[1]
    kall = jnp.concatenate([kb[:, :0:-1], (kf[:, :1] + kb[:, :1]), kf[:, 1:]], axis=1)
    kall_t = kall.transpose(0, 3, 1, 2).reshape(G, C, (2 * T - 1) * C)

    def pack_cols(re, im):
        z = jnp.zeros_like(re)
        odd = (jnp.arange(G) % 2 == 1)[:, None, None]
        return jnp.concatenate([jnp.where(odd, z, re), jnp.where(odd, re, z),
                                jnp.where(odd, z, im), jnp.where(odd, im, z)], axis=-1)

    vf_re = lb_re[0][:, :, T - 1::-1][:, :, :T].transpose(0, 2, 3, 1).reshape(G, T * C, P)
    vf_im = lb_im[0][:, :, T - 1::-1][:, :, :T].transpose(0, 2, 3, 1).reshape(G, T * C, P)
    vb_re = lb_re[1][:, :, :T].transpose(0, 2, 3, 1).reshape(G, T * C, P)
    vb_im = lb_im[1][:, :, :T].transpose(0, 2, 3, 1).reshape(G, T * C, P)
    v_f = pack_cols(vf_re, vf_im)
    v_b = pack_cols(vb_re, vb_im)
    wf_re = cl_re[0][..., 1:T + 1].transpose(0, 2, 3, 1).reshape(G, P, T * C)
    wf_im = cl_im[0][..., 1:T + 1].transpose(0, 2, 3, 1).reshape(G, P, T * C)
    wb_re = cl_re[1][..., T:0:-1].transpose(0, 2, 3, 1).reshape(G, P, T * C)
    wb_im = cl_im[1][..., T:0:-1].transpose(0, 2, 3, 1).reshape(G, P, T * C)
    pack_rows = lambda re, im: pack_cols(re.transpose(0, 2, 1), -im.transpose(0, 2, 1)).transpose(0, 2, 1)
    w_f = pack_rows(wf_re, wf_im)
    w_b = pack_rows(wb_re, wb_im)
    lam = jnp.stack([pr[0, :, :, T], pi[0, :, :, T], pr[1, :, :, T], pi[1, :, :, T]], axis=0)
    lam = lam.reshape(4, G // 2, 2 * P).transpose(1, 0, 2)
    return kall_t, _mm(v_f), _mm(v_b), _mm(w_f), _mm(w_b), lam


def _s5_pack_body(u_ref, o_ref):
    nbb = o_ref.shape[1]
    per_vreg = 128 // S5_GROUP
    for j in range(S5_T // per_vreg):
        slabs = [u_ref[pl.ds(j * per_vreg + t, nbb, stride=S5_T), :] for t in range(per_vreg)]
        for g in range(per_vreg):
            blk = jnp.concatenate([s[:, g * S5_GROUP:(g + 1) * S5_GROUP] for s in slabs], axis=1)
            o_ref[g, :, j * 128:(j + 1) * 128] = blk.astype(o_ref.dtype)


def _s5_unpack_body(y_ref, o_ref):
    nbb = y_ref.shape[1]
    per_vreg = 128 // S5_GROUP
    for j in range(S5_T // per_vreg):
        blks = [y_ref[g, :, j * 128:(j + 1) * 128] for g in range(per_vreg)]
        for t in range(per_vreg):
            slab = jnp.concatenate([b[:, t * S5_GROUP:(t + 1) * S5_GROUP] for b in blks], axis=1)
            o_ref[pl.ds(j * per_vreg + t, nbb, stride=S5_T), :] = slab


def _s5_inc_body(u_ref, vf_ref, vb_ref, incf_ref, incb_ref):
    incf_ref[...] = _dot(u_ref[0], vf_ref[0]) + _dot(u_ref[1], vf_ref[1])
    incb_ref[...] = _dot(u_ref[0], vb_ref[0]) + _dot(u_ref[1], vb_ref[1])


def _s5_scan_body(ifr_ref, ifi_ref, ibr_ref, ibi_ref, lam_ref, xfr_ref, xfi_ref, xbr_ref, xbi_ref, *, bsz, nc):
    lam = lam_ref[0]
    zero = jnp.zeros((bsz, 2 * S5_STATE), F32)
    m = nc - 1

    def run(inc_re, inc_im, x_re, x_im, lr, li, reverse):
        def step(i, carry):
            if reverse:
                j = jnp.where(i == m, m, m - 1 - i)
            else:
                j = jnp.where(i == 0, m, i - 1)
            rows = pl.ds(j, bsz, stride=nc)
            xr, xi = carry
            x_re[rows, :] = xr
            x_im[rows, :] = xi
            return (lr * xr - li * xi + inc_re[rows, :], lr * xi + li * xr + inc_im[rows, :])

        lax.fori_loop(0, nc, step, (zero, zero))

    run(ifr_ref, ifi_ref, xfr_ref, xfi_ref, lam[0:1], lam[1:2], False)
    run(ibr_ref, ibi_ref, xbr_ref, xbi_ref, lam[2:3], lam[3:4], True)


def _s5_out_body(u_ref, kall_ref, wf_ref, wb_ref, xfr_ref, xfi_ref, xbr_ref, xbi_ref, y_ref, m_ref):
    xf = _mm(jnp.concatenate([xfr_ref[...], xfi_ref[...]], axis=1))
    xb = _mm(jnp.concatenate([xbr_ref[...], xbi_ref[...]], axis=1))
    tw = S5_T * S5_GROUP
    for k in range(2):
        kall = kall_ref[k]
        for s in range(S5_T):
            off = (S5_T - 1 - s) * S5_GROUP
            m_ref[k, s * S5_GROUP:(s + 1) * S5_GROUP, :] = kall[:, off:off + tw].astype(m_ref.dtype)
        y_ref[k] = _dot(u_ref[k], m_ref[k]) + _dot(xf, wf_ref[k]) + _dot(xb, wb_ref[k])


def _s5_scan(u_a, ops, layer, bsz, n_tok):
    kall_t, v_f, v_b, w_f, w_b, lam = ops
    T, G, C = S5_T, S5_GROUPS, S5_GROUP
    l_pad = n_tok + BLK
    nc = l_pad // T
    rows = nc * bsz
    tw = T * C
    sw = 4 * S5_STATE
    nbb = S5_PACK_CHUNKS
    gh = 128 // C
    tok_half = pl.BlockSpec((nbb * T, 128), lambda i, h: (i, h))
    grp_half = pl.BlockSpec((gh, nbb, tw), lambda i, h: (h, i, 0))
    u = pl.pallas_call(
        _s5_pack_body,
        grid=(pl.cdiv(rows, nbb), G // gh),
        in_specs=[tok_half],
        out_specs=grp_half,
        out_shape=jax.ShapeDtypeStruct((G, rows, tw), MM_DTYPE),
        compiler_params=_params(("parallel", "parallel")),
        name="s5_pack",
    )(u_a)
    pair3 = lambda r, c: pl.BlockSpec((2, r, c), lambda p: (p, 0, 0))
    op3 = lambda r, c: pl.BlockSpec((None, 2, r, c), lambda p: (layer, p, 0, 0))
    lane = lambda r: pl.BlockSpec((r, sw), lambda p: (0, p))
    re = pl.BlockSpec((rows, sw // 2), lambda p: (0, 2 * p))
    im = pl.BlockSpec((rows, sw // 2), lambda p: (0, 2 * p + 1))
    half = pl.BlockSpec((rows, sw // 2), lambda p: (0, p))
    inc_f, inc_b = pl.pallas_call(
        _s5_inc_body,
        grid=(G // 2,),
        in_specs=[pair3(rows, tw), op3(tw, sw), op3(tw, sw)],
        out_specs=[lane(rows), lane(rows)],
        out_shape=[jax.ShapeDtypeStruct((rows, sw * G // 2), F32)] * 2,
        compiler_params=_params(("parallel",)),
        name="s5_inc",
    )(u, v_f, v_b)
    states = pl.pallas_call(
        functools.partial(_s5_scan_body, bsz=bsz, nc=nc),
        grid=(G // 2,),
        in_specs=[re, im, re, im, pl.BlockSpec((None, 1, 4, 2 * S5_STATE), lambda p: (layer, p, 0, 0))],
        out_specs=[half] * 4,
        out_shape=[jax.ShapeDtypeStruct((rows, sw * G // 4), F32)] * 4,
        compiler_params=_params(("parallel",)),
        name="s5_chunk_scan",
    )(inc_f, inc_f, inc_b, inc_b, lam)
    y = pl.pallas_call(
        _s5_out_body,
        grid=(G // 2,),
        in_specs=[pair3(rows, tw), op3(C, (2 * T - 1) * C), op3(sw, tw), op3(sw, tw)] + [half] * 4,
        out_specs=pair3(rows, tw),
        out_shape=jax.ShapeDtypeStruct((G, rows, tw), F32),
        scratch_shapes=[pltpu.VMEM((2, tw, tw), MM_DTYPE)],
        compiler_params=_params(("parallel",)),
        name="s5_out",
    )(u, kall_t, w_f, w_b, *states)
    return pl.pallas_call(
        _s5_unpack_body,
        grid=(pl.cdiv(rows, nbb), G // gh),
        in_specs=[grp_half],
        out_specs=tok_half,
        out_shape=jax.ShapeDtypeStruct((rows * T, S5_WIDTH), F32),
        compiler_params=_params(("parallel", "parallel")),
        name="s5_unpack",
    )(y)


def _na_bias(rpb):
    cols = np.arange(GRID_W)
    wstart = np.clip(cols - NA_COLS // 2, 0, GRID_W - NA_COLS)
    inwin = (cols[None, :] >= wstart[:, None]) & (cols[None, :] < wstart[:, None] + NA_COLS)
    dcol = np.clip(cols[None, :] - cols[:, None], -(NA_COLS - 1), NA_COLS - 1) + NA_COLS - 1
    drow = np.arange(NA_ROWS)[None, :] - np.arange(NA_ROWS)[:, None] + NA_ROWS - 1
    b = rpb.astype(F32)[:, drow][:, :, :, dcol]
    b = jnp.where(inwin[None, None, None], b, NEG)
    b = b.transpose(1, 0, 3, 2, 4)
    return b.reshape(NA_ROWS, NA_HEADS // NA_HG, NA_HG * GRID_W, NA_ROWS * GRID_W)


def _head_select(o, rows_per_head, lane_head):
    out = jnp.where(lane_head == 0, o[0:rows_per_head], 0.0)
    for h in range(1, NA_HG):
        out = out + jnp.where(lane_head == h, o[h * rows_per_head:(h + 1) * rows_per_head], 0.0)
    return out


def _na_body(q_ref, k_ref, v_ref, bias_ref, o_ref, *, n_tok):
    rg = pl.program_id(2)
    n_rg = n_tok // (NA_QROWS * GRID_W)
    grid_rows = n_tok // GRID_W
    meta0 = n_tok + PAD
    scale = NA_HEAD_DIM ** -0.5
    hw = NA_HG * NA_HEAD_DIM
    km = k_ref[0, meta0:meta0 + N_META, :]
    vm = v_ref[0, meta0:meta0 + N_META, :]

    def masked_queries(q, nq):
        row_head = lax.broadcasted_iota(jnp.int32, (NA_HG * nq, hw), 0) // nq
        lane_head = lax.broadcasted_iota(jnp.int32, (NA_HG * nq, hw), 1) // NA_HEAD_DIM
        qq = jnp.concatenate([q] * NA_HG, axis=0)
        return jnp.where(row_head == lane_head, qq, jnp.zeros_like(qq))

    @pl.when(rg < n_rg)
    def _grid_rows():
        lane_head = lax.broadcasted_iota(jnp.int32, (GRID_W, hw), 1) // NA_HEAD_DIM

        def row_step(rr, carry):
            r = rg * NA_QROWS + rr
            rs = jnp.clip(r - NA_ROWS // 2, 0, grid_rows - NA_ROWS)
            q0 = pl.multiple_of(rr * GRID_W, GRID_W)
            k0 = pl.multiple_of(rs * GRID_W, GRID_W)
            q = q_ref[0, pl.ds(q0, GRID_W), :] * scale
            qm = masked_queries(q.astype(q_ref.dtype), GRID_W)
            ks = k_ref[0, pl.ds(k0, NA_ROWS * GRID_W), :]
            vs = v_ref[0, pl.ds(k0, NA_ROWS * GRID_W), :]
            s = _dot_nt(qm, ks) + bias_ref[r - rs, 0]
            sm = _dot_nt(qm, km)
            fold = lambda x, op: functools.reduce(op, [x[:, i:i + 128] for i in range(0, x.shape[1], 128)])
            sm_wide = jnp.concatenate([sm, jnp.full((sm.shape[0], 128 - N_META), NEG, F32)], axis=1)
            mx = jnp.max(jnp.maximum(fold(s, jnp.maximum), sm_wide), axis=-1, keepdims=True)
            p = jnp.exp(s - mx)
            pm = jnp.exp(sm_wide - mx)
            den = jnp.sum(fold(p, jnp.add) + pm, axis=-1, keepdims=True)
            o = (_dot(_mm(p), vs) + _dot(_mm(pm[:, 0:N_META]), vm)) * (1.0 / den)
            o_ref[0, pl.ds(q0, GRID_W), :] = _head_select(o, GRID_W, lane_head).astype(o_ref.dtype)
            return carry

        lax.fori_loop(0, NA_QROWS, row_step, 0, unroll=True)

    @pl.when(rg == n_rg)
    def _meta_rows():
        lane_head = lax.broadcasted_iota(jnp.int32, (N_META, hw), 1) // NA_HEAD_DIM
        q = q_ref[0, PAD:PAD + N_META, :] * scale
        qm = masked_queries(q.astype(q_ref.dtype), N_META)
        s = _dot_nt(qm, km)
        p = jnp.exp(s - jnp.max(s, axis=-1, keepdims=True))
        o = _dot(_mm(p), vm) / jnp.sum(p, axis=-1, keepdims=True)
        o_ref[0, 0:PAD, :] = jnp.zeros((PAD, hw), o_ref.dtype)
        o_ref[0, PAD:BLK, :] = _head_select(o, N_META, lane_head).astype(o_ref.dtype)


def _neighborhood_attention(qkv, bias, layer, bsz, n_tok):
    l_pad = n_tok + BLK
    hw = NA_HG * NA_HEAD_DIM
    n_hg = NA_HEADS // NA_HG
    qrows = NA_QROWS * GRID_W
    whole = lambda off: pl.BlockSpec((1, l_pad, hw), lambda b, g, r: (b, 0, off + g), pipeline_mode=pl.Buffered(1))
    return pl.pallas_call(
        functools.partial(_na_body, n_tok=n_tok),
        grid=(bsz, n_hg, n_tok // qrows + 1),
        in_specs=[
            pl.BlockSpec((1, qrows, hw), lambda b, g, r: (b, r, g)),
            whole(n_hg),
            whole(2 * n_hg),
            pl.BlockSpec((None, NA_ROWS, 1, NA_HG * GRID_W, NA_ROWS * GRID_W), lambda b, g, r: (layer, 0, g, 0, 0)),
        ],
        out_specs=pl.BlockSpec((1, qrows, hw), lambda b, g, r: (b, r, g)),
        out_shape=jax.ShapeDtypeStruct((bsz, l_pad, NA_WIDTH), MM_DTYPE),
        compiler_params=_params(("parallel", "parallel", "arbitrary")),
        name="neighborhood_attention",
    )(qkv, qkv, qkv, bias)


def _hgrn_masks(rev):
    C, c = BLK, HG_SUB
    i = np.arange(C)
    s = np.arange(C // c)[:, None]
    d = np.arange(c)[:, None]
    keys_ok = (i[None] >= (s + 1) * c) if rev else (i[None] < s * c)
    diag_ok = (i[None] % c + d <= c - 1) if rev else (i[None] % c >= d)
    table = lambda ok: np.broadcast_to(np.where(ok, 0.0, NEG).astype(np.float32)[:, :, None], ok.shape + (HG_WIDTH,))
    return jnp.asarray(table(keys_ok)), jnp.asarray(table(diag_ok))


def _cumsum_rows(tri, x):
    hi = _mm(x)
    rest = x - hi.astype(F32)
    mid = _mm(rest)
    lo = _mm(rest - mid.astype(F32))
    w = x.shape[1]
    y = _dot(tri, jnp.concatenate([hi, mid, lo], axis=1))
    return y[:, 0:w] + y[:, w:2 * w] + y[:, 2 * w:3 * w]


def _hgrn_chunk(q, g2, lk, v, keys_ref, diag_ref, head_mask, hsum, st_ref, rev):
    C, c = BLK, HG_SUB
    nsub = C // c
    w = HG_WIDTH
    ri = lax.broadcasted_iota(jnp.int32, (C, C), 0)
    ci = lax.broadcasted_iota(jnp.int32, (C, C), 1)
    tri = jnp.where((ci >= ri) if rev else (ci <= ri), 1.0, 0.0).astype(MM_DTYPE)
    b2 = _cumsum_rows(tri, g2)
    kb = b2 - lk
    edge = b2[0:1] if rev else b2[C - 1:C]
    st = st_ref[...]
    acc = _dot_nt(_mm(q * jnp.exp2(b2)), _mm(st))
    d_t = _dot_tn(_mm(v), _mm(jnp.exp2(edge - kb)))
    st_ref[...] = st * jnp.exp2(edge) + d_t * head_mask
    vmm = _mm(v)
    row_head = lax.broadcasted_iota(jnp.int32, (HG_HEADS * c, w), 0) // c
    lane_head_q = lax.broadcasted_iota(jnp.int32, (HG_HEADS * c, w), 1) // HG_DK
    lane_head = lax.broadcasted_iota(jnp.int32, (c, w), 1) // HG_DK
    subs = range(nsub - 1) if rev else range(1, nsub)
    atts = []
    for s in subs:
        lo, hi = c * s, c * (s + 1)
        beta = b2[hi:hi + 1] if rev else b2[lo - 1:lo]
        qt = q[lo:hi] * jnp.exp2(b2[lo:hi] - beta)
        kt = jnp.exp2((beta - kb) + keys_ref[s])
        qh = jnp.where(row_head == lane_head_q, jnp.concatenate([qt] * HG_HEADS, axis=0), 0.0)
        atts.append(_dot_nt(_mm(qh), _mm(kt)))
    r = _dot(_mm(jnp.concatenate(atts, axis=0)), vmm)
    zero = jnp.zeros((c, w), F32)
    outs = []
    for n, s in enumerate(subs):
        o = zero
        for h in range(HG_HEADS):
            row = (n * HG_HEADS + h) * c
            o = o + jnp.where(lane_head == h, r[row:row + c], 0.0)
        outs.append(o)
    acc = acc + jnp.concatenate(outs + [zero] if rev else [zero] + outs, axis=0)
    ps = [_mm(q * jnp.exp2(lk))]
    for d in range(1, c):
        shift = C - d if rev else d
        ps.append(_mm(q * jnp.exp2((b2 - pltpu.roll(kb, shift, 0)) + diag_ref[d])))
    a = _dot(jnp.concatenate(ps, axis=0), hsum)
    acc = acc + a[0:C] * v
    for d in range(1, c):
        acc = acc + a[d * C:(d + 1) * C] * pltpu.roll(v, C - d if rev else d, 0)
    return acc


def _hgrn_body(qf_ref, gf_ref, lf_ref, vf_ref, qb_ref, gb_ref, lbk_ref, vb_ref, kf_ref, df_ref, kb_ref, db_ref,
               hm_ref, hs_ref, of_ref, ob_ref, stf_ref, stb_ref):
    @pl.when(pl.program_id(1) == 0)
    def _reset():
        stf_ref[...] = jnp.zeros_like(stf_ref)
        stb_ref[...] = jnp.zeros_like(stb_ref)

    consts = (hm_ref[...], hs_ref[...])
    for i in range(qf_ref.shape[0]):
        of_ref[i] = _hgrn_chunk(qf_ref[i], gf_ref[i], lf_ref[i], vf_ref[i], kf_ref, df_ref, *consts, stf_ref.at[i], False)
        ob_ref[i] = _hgrn_chunk(qb_ref[i], gb_ref[i], lbk_ref[i], vb_ref[i], kb_ref, db_ref, *consts, stb_ref.at[i], True)


def _hgrn_scan(hg, bsz, n_tok):
    l_pad = n_tok + BLK
    nb = l_pad // BLK
    m = nb - 1
    fwd = lambda n: (n + m) % nb
    bwd = lambda n: jnp.where(n == m, m, m - 1 - n)
    hb = max(d for d in range(1, HG_BATCH + 1) if bsz % d == 0)
    spec = lambda order, col: pl.BlockSpec((hb, BLK, HG_WIDTH), lambda b, n: (b, order(n), col))
    masks = _hgrn_masks(False) + _hgrn_masks(True) + (_head_block_mask(F32), _head_block_mask(MM_DTYPE))
    return pl.pallas_call(
        _hgrn_body,
        grid=(bsz // hb, nb),
        in_specs=[spec(fwd, 0), spec(fwd, 1), spec(fwd, 5), spec(fwd, 3),
                  spec(bwd, 0), spec(bwd, 2), spec(bwd, 6), spec(bwd, 3)] + [_const_spec(t.shape) for t in masks],
        out_specs=[spec(fwd, 0), spec(bwd, 0)],
        out_shape=[jax.ShapeDtypeStruct((bsz, l_pad, HG_WIDTH), F32)] * 2,
        scratch_shapes=[pltpu.VMEM((hb, HG_WIDTH, HG_WIDTH), F32)] * 2,
        compiler_params=_params(("parallel", "arbitrary")),
        name="hgrn2_scan",
    )(hg, hg, hg, hg, hg, hg, hg, hg, *masks)


def _out_ffn_body(h_ref, ys_ref, ua_ref, att_ref, of_ref, ob_ref, gc_ref, gt_ref,
                  d_ref, wglu_ref, on_ref, hmean_ref, wa_ref, wb_ref, wc_ref, wo_ref, n2_ref, wg_ref, wu_ref, wd_ref,
                  out_ref, *, blocks_per_seq):
    tm = h_ref.shape[0]
    y = jax.nn.gelu(ys_ref[...] + d_ref[...] * ua_ref[...])
    y = y * _sigmoid(_dot(_mm(y), wglu_ref[...]))
    o = of_ref[...] + ob_ref[...]
    sq = o * o
    sq_hi = _mm(sq)
    ms = _dot(sq_hi, hmean_ref[...]) + _dot(_mm(sq - sq_hi.astype(F32)), hmean_ref[...])
    o = o * lax.rsqrt(ms + EPS) * on_ref[...]
    gc = gc_ref[...]
    o = o * (gc * _sigmoid(gc))
    mix = (_sigmoid(gt_ref[:, 0:D_MODEL]) * _dot(_mm(y), wa_ref[...])
           + _sigmoid(gt_ref[:, D_MODEL:2 * D_MODEL]) * _dot(att_ref[...], wb_ref[...])
           + _sigmoid(gt_ref[:, 2 * D_MODEL:3 * D_MODEL]) * _dot(_mm(o), wc_ref[...]))
    h = h_ref[...] + _dot(_mm(mix), wo_ref[...])
    hn = _mm(h * lax.rsqrt(jnp.mean(h * h, axis=-1, keepdims=True) + EPS) * n2_ref[...])
    for c0 in range(0, FFN_HIDDEN, FFN_CHUNK):
        gate = _dot(hn, wg_ref[:, c0:c0 + FFN_CHUNK])
        up = _dot(hn, wu_ref[:, c0:c0 + FFN_CHUNK])
        h = h + _dot(_mm(gate * _sigmoid(gate) * up), wd_ref[c0:c0 + FFN_CHUNK, :])
    out_ref[...] = h
    first_blk = pl.program_id(0) * (tm // BLK)
    for j in range(tm // BLK):
        @pl.when((first_blk + j) % blocks_per_seq == blocks_per_seq - 1)
        def _zero_rows():
            out_ref[j * BLK:j * BLK + PAD, :] = jnp.zeros((PAD, D_MODEL), F32)


def _out_ffn(h2d, ys, ua, att, o_f, o_b, hg, gates, lw, blocks_per_seq):
    rows = h2d.shape[0]
    tm = ROW_TILE_OUT
    row = lambda width, col=0: pl.BlockSpec((tm, width), lambda i: (i, col))
    weights = [lw["d"], lw["w_glu"], lw["onorm"], _head_block_mask(MM_DTYPE, 1.0 / HG_DK),
               lw["w_up_a"], lw["w_up_b"], lw["w_up_c"], lw["w_o"],
               lw["norm2"], lw["w_ffn_gate"], lw["w_ffn_up"], lw["w_ffn_down"]]
    return pl.pallas_call(
        functools.partial(_out_ffn_body, blocks_per_seq=blocks_per_seq),
        grid=(pl.cdiv(rows, tm),),
        in_specs=[row(D_MODEL), row(S5_WIDTH), row(S5_WIDTH), row(NA_WIDTH), row(HG_WIDTH), row(HG_WIDTH),
                  row(HG_WIDTH, 4), row(GATE_W)] + [_const_spec(w.shape) for w in weights],
        out_specs=row(D_MODEL),
        out_shape=jax.ShapeDtypeStruct((rows, D_MODEL), F32),
        compiler_params=_params(("parallel",)),
        name="mix_out_ffn",
    )(h2d, ys, ua, att, o_f, o_b, hg, gates, *weights)


def _final_norm_body(h_ref, g_ref, o_ref):
    x = h_ref[0]
    o_ref[0] = x * lax.rsqrt(jnp.mean(x * x, axis=-1, keepdims=True) + EPS) * g_ref[...]


def _final_norm(h, g, n_tok):
    bsz = h.shape[0]
    tl = 512
    spec = pl.BlockSpec((1, tl, D_MODEL), lambda b, j: (b, j, 0))
    return pl.pallas_call(
        _final_norm_body,
        grid=(bsz, n_tok // tl),
        in_specs=[spec, pl.BlockSpec((1, D_MODEL), lambda b, j: (0, 0))],
        out_specs=spec,
        out_shape=jax.ShapeDtypeStruct((bsz, n_tok, D_MODEL), F32),
        compiler_params=_params(("parallel", "parallel")),
        name="final_norm",
    )(h, g)


def _trunk(x, meta_tokens, layers, s5_ops, na_bias, final_g):
    bsz, n_tok, _ = x.shape
    l_pad = n_tok + BLK
    rows = bsz * l_pad
    meta = jnp.broadcast_to(meta_tokens[None].astype(F32), (bsz, N_META, D_MODEL))
    h = jnp.concatenate([x.astype(F32), jnp.zeros((bsz, PAD, D_MODEL), F32), meta], axis=1).reshape(rows, D_MODEL)
    for l, lw in enumerate(layers):
        ua, qkv, hg, gates = _in_proj(h, lw["norm1"], lw["w_in"], lw["lb"])
        ys = _s5_scan(ua, s5_ops, l, bsz, n_tok)
        att = _neighborhood_attention(qkv.reshape(bsz, l_pad, QKV_W), na_bias, l, bsz, n_tok)
        o_f, o_b = _hgrn_scan(hg.reshape(bsz, l_pad, HG_IN_W), bsz, n_tok)
        h = _out_ffn(h, ys, ua, att.reshape(rows, NA_WIDTH), o_f.reshape(rows, HG_WIDTH),
                     o_b.reshape(rows, HG_WIDTH), hg, gates, lw, l_pad // BLK)
    return _final_norm(h.reshape(bsz, l_pad, D_MODEL), final_g, n_tok)


def kernel(x_prompt, x_sample, meta_tokens, norm1_g, w_in, s5_a_re, s5_a_im, s5_log_dt, s5_b_re, s5_b_im,
           s5_c_re, s5_c_im, s5_d, s5_w_glu, na_rpb, hg_lb_logits, hg_onorm_g, w_up_a, w_up_b, w_up_c,
           w_o, norm2_g, w_ffn_gate, w_ffn_up, w_ffn_down, final_norm_g):
    depth = w_in.shape[0]
    sm = jax.nn.softmax(hg_lb_logits.astype(F32), axis=0)
    lbs = jnp.cumsum(sm, axis=0) - sm[0:1]
    layers = []
    for l in range(depth):
        layers.append(dict(
            norm1=norm1_g[l].astype(F32)[None],
            w_in=_mm(w_in[l]),
            lb=lbs[l][None],
            d=s5_d[l].astype(F32)[None],
            w_glu=_mm(s5_w_glu[l]),
            onorm=jnp.tile(hg_onorm_g[l].astype(F32), HG_HEADS)[None],
            w_up_a=_mm(w_up_a[l]), w_up_b=_mm(w_up_b[l]), w_up_c=_mm(w_up_c[l]), w_o=_mm(w_o[l]),
            norm2=norm2_g[l].astype(F32)[None],
            w_ffn_gate=_mm(w_ffn_gate[l]), w_ffn_up=_mm(w_ffn_up[l]), w_ffn_down=_mm(w_ffn_down[l]),
        ))
    s5_ops = jax.vmap(_s5_operators)(*(t.astype(F32) for t in (s5_a_re, s5_a_im, s5_log_dt, s5_b_re, s5_b_im,
                                                               s5_c_re, s5_c_im)))
    na_bias = jax.vmap(_na_bias)(na_rpb)
    final_g = final_norm_g.astype(F32)[None]
    return tuple(_trunk(x, meta_tokens, layers, s5_ops, na_bias, final_g) for x in (x_prompt, x_sample))
```

```python
import functools

import numpy as np
import jax
import jax.numpy as jnp
from jax import lax
from jax.experimental import pallas as pl
from jax.experimental.pallas import tpu as pltpu

F32 = jnp.float32
MM_DTYPE = jnp.bfloat16

D_MODEL = 1024
N_META = 16
GRID_W = 64
EPS = 1e-6
NEG = -1e30
BLK = 64
PAD = BLK - N_META
S5_WIDTH = 256
S5_GROUP = 16
S5_GROUPS = 16
S5_STATE = 64
S5_T = 64
S5_PACK_CHUNKS = 16
NA_HEADS = 8
NA_HEAD_DIM = 64
NA_WIDTH = 512
NA_ROWS = 8
NA_COLS = 16
NA_HG = 4
NA_QROWS = 32
HG_HEADS = 4
HG_DK = 64
HG_WIDTH = 256
HG_SUB = 8
HG_BATCH = 8
FFN_HIDDEN = 2816
FFN_CHUNK = 2816
IN_WIDTH = 6144
QKV_W = 3 * NA_WIDTH
HG_IN_W = 7 * HG_WIDTH
GATE_W = 3 * D_MODEL
ROW_TILE_IN = 256
ROW_TILE_OUT = 512
VMEM_LIMIT = 56 * 1024 * 1024


def _dot(a, b):
    return jnp.dot(a, b, preferred_element_type=F32)


def _dot_nt(a, b):
    return lax.dot_general(a, b, (((1,), (1,)), ((), ())), preferred_element_type=F32)


def _dot_tn(a, b):
    return lax.dot_general(a, b, (((0,), (0,)), ((), ())), preferred_element_type=F32)


def _dot_f32(a, b):
    return jnp.dot(a, b, preferred_element_type=F32, precision=lax.Precision.HIGHEST)


def _mm(x):
    return x.astype(MM_DTYPE)


def _sigmoid(x):
    return 0.5 * jnp.tanh(0.5 * x) + 0.5


def _head_block_mask(dtype, scale=1.0):
    r = np.arange(HG_WIDTH) // HG_DK
    return jnp.asarray(np.where(r[:, None] == r[None, :], scale, 0.0), dtype)


def _const_spec(shape):
    nd = len(shape)
    return pl.BlockSpec(shape, lambda *_: (0,) * nd, pipeline_mode=pl.Buffered(1))


def _params(sem):
    return pltpu.CompilerParams(dimension_semantics=sem, vmem_limit_bytes=VMEM_LIMIT)


def _in_proj_body(h_ref, g_ref, w_ref, lb_ref, ua_ref, qkv_ref, hg_ref, gt_ref):
    x = h_ref[...]
    xn = x * lax.rsqrt(jnp.mean(x * x, axis=-1, keepdims=True) + EPS) * g_ref[...]
    xn = _mm(xn)
    w_hg = HG_WIDTH
    c_qkv = S5_WIDTH
    c_hg = c_qkv + QKV_W
    c_gt = c_hg + 5 * w_hg
    z = _dot(xn, w_ref[:, c_hg:c_gt])
    q = z[:, 0:w_hg]
    hg_ref[:, 0:w_hg] = q * _sigmoid(q)
    hg_ref[:, 3 * w_hg:5 * w_hg] = z[:, 3 * w_hg:5 * w_hg]
    lb = lb_ref[...]
    for src, dst in ((1, 5), (2, 6)):
        sig = jax.nn.sigmoid(z[:, src * w_hg:(src + 1) * w_hg])
        hg_ref[:, src * w_hg:(src + 1) * w_hg] = jnp.log2(lb + (1.0 - lb) * sig)
        hg_ref[:, dst * w_hg:(dst + 1) * w_hg] = jnp.log2((1.0 - lb) * (1.0 - sig))
    ua_ref[...] = _dot(xn, w_ref[:, 0:c_qkv])
    qkv_ref[...] = _dot(xn, w_ref[:, c_qkv:c_hg]).astype(qkv_ref.dtype)
    gt_ref[...] = _dot(xn, w_ref[:, c_gt:c_gt + GATE_W])


def _in_proj(h2d, g, w, lb):
    rows = h2d.shape[0]
    tm = ROW_TILE_IN
    row = lambda width: pl.BlockSpec((tm, width), lambda i: (i, 0))
    return pl.pallas_call(
        _in_proj_body,
        grid=(pl.cdiv(rows, tm),),
        in_specs=[row(D_MODEL), _const_spec((1, D_MODEL)), _const_spec((D_MODEL, IN_WIDTH)), _const_spec((1, HG_WIDTH))],
        out_specs=[row(S5_WIDTH), row(QKV_W), row(HG_IN_W), row(GATE_W)],
        out_shape=[
            jax.ShapeDtypeStruct((rows, S5_WIDTH), F32),
            jax.ShapeDtypeStruct((rows, QKV_W), MM_DTYPE),
            jax.ShapeDtypeStruct((rows, HG_IN_W), F32),
            jax.ShapeDtypeStruct((rows, GATE_W), F32),
        ],
        compiler_params=_params(("parallel",)),
        name="in_proj",
    )(h2d, g, w, lb)


def _s5_operators(a_re, a_im, log_dt, b_re, b_im, c_re, c_im):
    T, G, P, C = S5_T, S5_GROUPS, S5_STATE, S5_GROUP
    hi = lax.Precision.HIGHEST
    dt = jnp.exp(log_dt)[..., None]
    mag = jnp.exp(a_re * dt)
    lr = mag * jnp.cos(a_im * dt)
    li = mag * jnp.sin(a_im * dt)
    den = a_re * a_re + a_im * a_im
    nr = lr - 1.0
    z_re = (nr * a_re + li * a_im) / den
    z_im = (li * a_re - nr * a_im) / den
    bb_re = z_re[..., None] * b_re - z_im[..., None] * b_im
    bb_im = z_re[..., None] * b_im + z_im[..., None] * b_re
    d = jnp.arange(T + 1, dtype=F32)
    pmag = jnp.exp((a_re * dt)[..., None] * d)
    pr = pmag * jnp.cos((a_im * dt)[..., None] * d)
    pi = pmag * jnp.sin((a_im * dt)[..., None] * d)
    lb_re = pr[..., None] * bb_re[:, :, :, None, :] - pi[..., None] * bb_im[:, :, :, None, :]
    lb_im = pr[..., None] * bb_im[:, :, :, None, :] + pi[..., None] * bb_re[:, :, :, None, :]
    cl_re = c_re[..., None] * pr[:, :, None] - c_im[..., None] * pi[:, :, None]
    cl_im = c_re[..., None] * pi[:, :, None] + c_im[..., None] * pr[:, :, None]
    kk = (jnp.einsum('xgop,xgpdi->xgdoi', c_re, lb_re, precision=hi)
          - jnp.einsum('xgop,xgpdi->xgdoi', c_im, lb_im, precision=hi))[:, :, :T]
    kf, kb = kk[0], kk[1]
    kall = jnp.concatenate([kb[:, :0:-1], (kf[:, :1] + kb[:, :1]), kf[:, 1:]], axis=1)
    kall_t = kall.transpose(0, 3, 1, 2).reshape(G, C, (2 * T - 1) * C)

    def pack_cols(re, im):
        z = jnp.zeros_like(re)
        odd = (jnp.arange(G) % 2 == 1)[:, None, None]
        return jnp.concatenate([jnp.where(odd, z, re), jnp.where(odd, re, z),
                                jnp.where(odd, z, im), jnp.where(odd, im, z)], axis=-1)

    vf_re = lb_re[0][:, :, T - 1::-1][:, :, :T].transpose(0, 2, 3, 1).reshape(G, T * C, P)
    vf_im = lb_im[0][:, :, T - 1::-1][:, :, :T].transpose(0, 2, 3, 1).reshape(G, T * C, P)
    vb_re = lb_re[1][:, :, :T].transpose(0, 2, 3, 1).reshape(G, T * C, P)
    vb_im = lb_im[1][:, :, :T].transpose(0, 2, 3, 1).reshape(G, T * C, P)
    def slot_rows(v):
        n = 128 // C
        src = (np.arange(n)[None, :] - np.arange(G)[:, None]) % n
        v5 = v.reshape(G, T // n, n, C, v.shape[-1])
        return jnp.take_along_axis(v5, jnp.asarray(src)[:, None, :, None, None], axis=2).reshape(v.shape)

    v_f = slot_rows(pack_cols(vf_re, vf_im))
    v_b = slot_rows(pack_cols(vb_re, vb_im))
    wf_re = cl_re[0][..., 1:T + 1].transpose(0, 2, 3, 1).reshape(G, P, T * C)
    wf_im = cl_im[0][..., 1:T + 1].transpose(0, 2, 3, 1).reshape(G, P, T * C)
    wb_re = cl_re[1][..., T:0:-1].transpose(0, 2, 3, 1).reshape(G, P, T * C)
    wb_im = cl_im[1][..., T:0:-1].transpose(0, 2, 3, 1).reshape(G, P, T * C)
    pack_rows = lambda re, im: pack_cols(re.transpose(0, 2, 1), -im.transpose(0, 2, 1)).transpose(0, 2, 1)
    w_f = pack_rows(wf_re, wf_im)
    w_b = pack_rows(wb_re, wb_im)
    lam = jnp.stack([pr[0, :, :, T], pi[0, :, :, T], pr[1, :, :, T], pi[1, :, :, T]], axis=0)
    lam = lam.reshape(4, G // 2, 2 * P).transpose(1, 0, 2)
    return kall_t, _mm(v_f), _mm(v_b), _mm(w_f), _mm(w_b), lam


def _s5_pack_body(u_ref, o_ref):
    nbb = o_ref.shape[1]
    per_vreg = 128 // S5_GROUP
    slot = lax.broadcasted_iota(jnp.int32, (nbb, 128), 1) // S5_GROUP
    for j in range(S5_T // per_vreg):
        slabs = [u_ref[pl.ds(j * per_vreg + t, nbb, stride=S5_T), :] for t in range(per_vreg)]
        slabs = [s if t == 0 else pltpu.roll(s, t * S5_GROUP, 1) for t, s in enumerate(slabs)]
        for g in range(per_vreg):
            blk = slabs[0]
            for t in range(1, per_vreg):
                blk = jnp.where(slot == (t + g) % per_vreg, slabs[t], blk)
            o_ref[g, :, j * 128:(j + 1) * 128] = blk.astype(o_ref.dtype)


def _s5_unpack_body(y_ref, o_ref):
    nbb = y_ref.shape[1]
    per_vreg = 128 // S5_GROUP
    slot = lax.broadcasted_iota(jnp.int32, (nbb, 128), 1) // S5_GROUP
    for j in range(S5_T // per_vreg):
        blks = [y_ref[g, :, j * 128:(j + 1) * 128] for g in range(per_vreg)]
        for t in range(per_vreg):
            slab = blks[0]
            for g in range(1, per_vreg):
                slab = jnp.where(slot == (t + g) % per_vreg, blks[g], slab)
            if t:
                slab = pltpu.roll(slab, 128 - t * S5_GROUP, 1)
            o_ref[pl.ds(j * per_vreg + t, nbb, stride=S5_T), :] = slab


def _s5_inc_body(u_ref, vf_ref, vb_ref, incf_ref, incb_ref):
    incf_ref[...] = _dot(u_ref[0], vf_ref[0]) + _dot(u_ref[1], vf_ref[1])
    incb_ref[...] = _dot(u_ref[0], vb_ref[0]) + _dot(u_ref[1], vb_ref[1])


def _s5_scan_body(ifr_ref, ifi_ref, ibr_ref, ibi_ref, lam_ref, xfr_ref, xfi_ref, xbr_ref, xbi_ref, *, bsz, nc):
    lam = lam_ref[0]
    zero = jnp.zeros((bsz, 2 * S5_STATE), F32)
    m = nc - 1

    def run(inc_re, inc_im, x_re, x_im, lr, li, reverse):
        def step(i, carry):
            if reverse:
                j = jnp.where(i == m, m, m - 1 - i)
            else:
                j = jnp.where(i == 0, m, i - 1)
            rows = pl.ds(j, bsz, stride=nc)
            xr, xi = carry
            x_re[rows, :] = xr
            x_im[rows, :] = xi
            return (lr * xr - li * xi + inc_re[rows, :], lr * xi + li * xr + inc_im[rows, :])

        lax.fori_loop(0, nc, step, (zero, zero))

    run(ifr_ref, ifi_ref, xfr_ref, xfi_ref, lam[0:1], lam[1:2], False)
    run(ibr_ref, ibi_ref, xbr_ref, xbi_ref, lam[2:3], lam[3:4], True)


def _s5_out_body(u_ref, kall_ref, wf_ref, wb_ref, xfr_ref, xfi_ref, xbr_ref, xbi_ref, y_ref, m_ref):
    xf = _mm(jnp.concatenate([xfr_ref[...], xfi_ref[...]], axis=1))
    xb = _mm(jnp.concatenate([xbr_ref[...], xbi_ref[...]], axis=1))
    tw = S5_T * S5_GROUP
    per_vreg = 128 // S5_GROUP
    for k in range(2):
        kall = kall_ref[k]
        g = 2 * pl.program_id(0) + k
        for s in range(S5_T):
            off = (S5_T - 1 - s) * S5_GROUP
            slot = (s // per_vreg) * per_vreg + (s % per_vreg + g) % per_vreg
            row0 = pl.multiple_of(slot * S5_GROUP, S5_GROUP)
            m_ref[k, pl.ds(row0, S5_GROUP), :] = kall[:, off:off + tw].astype(m_ref.dtype)
        y = _dot(u_ref[k], m_ref[k]) + _dot(xf, wf_ref[k]) + _dot(xb, wb_ref[k])
        shift = (g % per_vreg) * S5_GROUP
        for b in range(tw // 128):
            y_ref[k, :, b * 128:(b + 1) * 128] = pltpu.roll(y[:, b * 128:(b + 1) * 128], shift, 1)


def _s5_scan(u_a, ops, layer, bsz, n_tok):
    kall_t, v_f, v_b, w_f, w_b, lam = ops
    T, G, C = S5_T, S5_GROUPS, S5_GROUP
    l_pad = n_tok + BLK
    nc = l_pad // T
    rows = nc * bsz
    tw = T * C
    sw = 4 * S5_STATE
    nbb = S5_PACK_CHUNKS
    gh = 128 // C
    tok_half = pl.BlockSpec((nbb * T, 128), lambda i, h: (i, h))
    grp_half = pl.BlockSpec((gh, nbb, tw), lambda i, h: (h, i, 0))
    u = pl.pallas_call(
        _s5_pack_body,
        grid=(pl.cdiv(rows, nbb), G // gh),
        in_specs=[tok_half],
        out_specs=grp_half,
        out_shape=jax.ShapeDtypeStruct((G, rows, tw), MM_DTYPE),
        compiler_params=_params(("parallel", "parallel")),
        name="s5_pack",
    )(u_a)
    pair3 = lambda r, c: pl.BlockSpec((2, r, c), lambda p: (p, 0, 0))
    op3 = lambda r, c: pl.BlockSpec((None, 2, r, c), lambda p: (layer, p, 0, 0))
    lane = lambda r: pl.BlockSpec((r, sw), lambda p: (0, p))
    re = pl.BlockSpec((rows, sw // 2), lambda p: (0, 2 * p))
    im = pl.BlockSpec((rows, sw // 2), lambda p: (0, 2 * p + 1))
    half = pl.BlockSpec((rows, sw // 2), lambda p: (0, p))
    inc_f, inc_b = pl.pallas_call(
        _s5_inc_body,
        grid=(G // 2,),
        in_specs=[pair3(rows, tw), op3(tw, sw), op3(tw, sw)],
        out_specs=[lane(rows), lane(rows)],
        out_shape=[jax.ShapeDtypeStruct((rows, sw * G // 2), F32)] * 2,
        compiler_params=_params(("parallel",)),
        name="s5_inc",
    )(u, v_f, v_b)
    states = pl.pallas_call(
        functools.partial(_s5_scan_body, bsz=bsz, nc=nc),
        grid=(G // 2,),
        in_specs=[re, im, re, im, pl.BlockSpec((None, 1, 4, 2 * S5_STATE), lambda p: (layer, p, 0, 0))],
        out_specs=[half] * 4,
        out_shape=[jax.ShapeDtypeStruct((rows, sw * G // 4), F32)] * 4,
        compiler_params=_params(("parallel",)),
        name="s5_chunk_scan",
    )(inc_f, inc_f, inc_b, inc_b, lam)
    y = pl.pallas_call(
        _s5_out_body,
        grid=(G // 2,),
        in_specs=[pair3(rows, tw), op3(C, (2 * T - 1) * C), op3(sw, tw), op3(sw, tw)] + [half] * 4,
        out_specs=pair3(rows, tw),
        out_shape=jax.ShapeDtypeStruct((G, rows, tw), F32),
        scratch_shapes=[pltpu.VMEM((2, tw, tw), MM_DTYPE)],
        compiler_params=_params(("parallel",)),
        name="s5_out",
    )(u, kall_t, w_f, w_b, *states)
    return pl.pallas_call(
        _s5_unpack_body,
        grid=(pl.cdiv(rows, nbb), G // gh),
        in_specs=[grp_half],
        out_specs=tok_half,
        out_shape=jax.ShapeDtypeStruct((rows * T, S5_WIDTH), F32),
        compiler_params=_params(("parallel", "parallel")),
        name="s5_unpack",
    )(y)


def _na_bias(rpb):
    cols = np.arange(GRID_W)
    wstart = np.clip(cols - NA_COLS // 2, 0, GRID_W - NA_COLS)
    inwin = (cols[None, :] >= wstart[:, None]) & (cols[None, :] < wstart[:, None] + NA_COLS)
    dcol = np.clip(cols[None, :] - cols[:, None], -(NA_COLS - 1), NA_COLS - 1) + NA_COLS - 1
    drow = np.arange(NA_ROWS)[None, :] - np.arange(NA_ROWS)[:, None] + NA_ROWS - 1
    b = rpb.astype(F32)[:, drow][:, :, :, dcol]
    b = jnp.where(inwin[None, None, None], b, NEG)
    b = b.transpose(1, 0, 3, 2, 4)
    return b.reshape(NA_ROWS, NA_HEADS // NA_HG, NA_HG * GRID_W, NA_ROWS * GRID_W)


def _head_select(o, rows_per_head, lane_head):
    out = jnp.where(lane_head == 0, o[0:rows_per_head], 0.0)
    for h in range(1, NA_HG):
        out = out + jnp.where(lane_head == h, o[h * rows_per_head:(h + 1) * rows_per_head], 0.0)
    return out


def _na_body(q_ref, k_ref, v_ref, bias_ref, o_ref, *, n_tok):
    rg = pl.program_id(2)
    n_rg = n_tok // (NA_QROWS * GRID_W)
    grid_rows = n_tok // GRID_W
    meta0 = n_tok + PAD
    scale = NA_HEAD_DIM ** -0.5
    hw = NA_HG * NA_HEAD_DIM
    km = k_ref[0, meta0:meta0 + N_META, :]
    vm = v_ref[0, meta0:meta0 + N_META, :]

    def masked_queries(q, nq):
        row_head = lax.broadcasted_iota(jnp.int32, (NA_HG * nq, hw), 0) // nq
        lane_head = lax.broadcasted_iota(jnp.int32, (NA_HG * nq, hw), 1) // NA_HEAD_DIM
        qq = jnp.concatenate([q] * NA_HG, axis=0)
        return jnp.where(row_head == lane_head, qq, jnp.zeros_like(qq))

    @pl.when(rg < n_rg)
    def _grid_rows():
        lane_head = lax.broadcasted_iota(jnp.int32, (GRID_W, hw), 1) // NA_HEAD_DIM

        def row_step(rr, carry):
            r = rg * NA_QROWS + rr
            rs = jnp.clip(r - NA_ROWS // 2, 0, grid_rows - NA_ROWS)
            q0 = pl.multiple_of(rr * GRID_W, GRID_W)
            k0 = pl.multiple_of(rs * GRID_W, GRID_W)
            q = q_ref[0, pl.ds(q0, GRID_W), :] * scale
            qm = masked_queries(q.astype(q_ref.dtype), GRID_W)
            ks = k_ref[0, pl.ds(k0, NA_ROWS * GRID_W), :]
            vs = v_ref[0, pl.ds(k0, NA_ROWS * GRID_W), :]
            s = _dot_nt(qm, ks) + bias_ref[r - rs, 0]
            sm = _dot_nt(qm, km)
            fold = lambda x, op: functools.reduce(op, [x[:, i:i + 128] for i in range(0, x.shape[1], 128)])
            sm_wide = jnp.concatenate([sm, jnp.full((sm.shape[0], 128 - N_META), NEG, F32)], axis=1)
            mx = jnp.max(jnp.maximum(fold(s, jnp.maximum), sm_wide), axis=-1, keepdims=True)
            p = jnp.exp(s - mx)
            pm = jnp.exp(sm_wide - mx)
            den = jnp.sum(fold(p, jnp.add) + pm, axis=-1, keepdims=True)
            o = (_dot(_mm(p), vs) + _dot(_mm(pm[:, 0:N_META]), vm)) * (1.0 / den)
            o_ref[0, pl.ds(q0, GRID_W), :] = _head_select(o, GRID_W, lane_head).astype(o_ref.dtype)
            return carry

        lax.fori_loop(0, NA_QROWS, row_step, 0, unroll=True)

    @pl.when(rg == n_rg)
    def _meta_rows():
        lane_head = lax.broadcasted_iota(jnp.int32, (N_META, hw), 1) // NA_HEAD_DIM
        q = q_ref[0, PAD:PAD + N_META, :] * scale
        qm = masked_queries(q.astype(q_ref.dtype), N_META)
        s = _dot_nt(qm, km)
        p = jnp.exp(s - jnp.max(s, axis=-1, keepdims=True))
        o = _dot(_mm(p), vm) / jnp.sum(p, axis=-1, keepdims=True)
        o_ref[0, 0:PAD, :] = jnp.zeros((PAD, hw), o_ref.dtype)
        o_ref[0, PAD:BLK, :] = _head_select(o, N_META, lane_head).astype(o_ref.dtype)


def _neighborhood_attention(qkv, bias, layer, bsz, n_tok):
    l_pad = n_tok + BLK
    hw = NA_HG * NA_HEAD_DIM
    n_hg = NA_HEADS // NA_HG
    qrows = NA_QROWS * GRID_W
    whole = lambda off: pl.BlockSpec((1, l_pad, hw), lambda b, g, r: (b, 0, off + g), pipeline_mode=pl.Buffered(1))
    return pl.pallas_call(
        functools.partial(_na_body, n_tok=n_tok),
        grid=(bsz, n_hg, n_tok // qrows + 1),
        in_specs=[
            pl.BlockSpec((1, qrows, hw), lambda b, g, r: (b, r, g)),
            whole(n_hg),
            whole(2 * n_hg),
            pl.BlockSpec((None, NA_ROWS, 1, NA_HG * GRID_W, NA_ROWS * GRID_W), lambda b, g, r: (layer, 0, g, 0, 0)),
        ],
        out_specs=pl.BlockSpec((1, qrows, hw), lambda b, g, r: (b, r, g)),
        out_shape=jax.ShapeDtypeStruct((bsz, l_pad, NA_WIDTH), MM_DTYPE),
        compiler_params=_params(("parallel", "parallel", "arbitrary")),
        name="neighborhood_attention",
    )(qkv, qkv, qkv, bias)


def _hgrn_masks(rev):
    C, c = BLK, HG_SUB
    i = np.arange(C)
    s = np.arange(C // c)[:, None]
    d = np.arange(c)[:, None]
    keys_ok = (i[None] >= (s + 1) * c) if rev else (i[None] < s * c)
    diag_ok = (i[None] % c + d <= c - 1) if rev else (i[None] % c >= d)
    table = lambda ok: np.broadcast_to(np.where(ok, 0.0, NEG).astype(np.float32)[:, :, None], ok.shape + (HG_WIDTH,))
    return jnp.asarray(table(keys_ok)), jnp.asarray(table(diag_ok))


def _cumsum_rows(tri, x):
    hi = _mm(x)
    rest = x - hi.astype(F32)
    mid = _mm(rest)
    lo = _mm(rest - mid.astype(F32))
    w = x.shape[1]
    y = _dot(tri, jnp.concatenate([hi, mid, lo], axis=1))
    return y[:, 0:w] + y[:, w:2 * w] + y[:, 2 * w:3 * w]


def _hgrn_chunk(q, g2, lk, v, keys_ref, diag_ref, head_mask, hsum, st_ref, rev):
    C, c = BLK, HG_SUB
    nsub = C // c
    w = HG_WIDTH
    ri = lax.broadcasted_iota(jnp.int32, (C, C), 0)
    ci = lax.broadcasted_iota(jnp.int32, (C, C), 1)
    tri = jnp.where((ci >= ri) if rev else (ci <= ri), 1.0, 0.0).astype(MM_DTYPE)
    b2 = _cumsum_rows(tri, g2)
    kb = b2 - lk
    edge = b2[0:1] if rev else b2[C - 1:C]
    st = st_ref[...]
    acc = _dot_nt(_mm(q * jnp.exp2(b2)), _mm(st))
    d_t = _dot_tn(_mm(v), _mm(jnp.exp2(edge - kb)))
    st_ref[...] = st * jnp.exp2(edge) + d_t * head_mask
    vmm = _mm(v)
    row_head = lax.broadcasted_iota(jnp.int32, (HG_HEADS * c, w), 0) // c
    lane_head_q = lax.broadcasted_iota(jnp.int32, (HG_HEADS * c, w), 1) // HG_DK
    lane_head = lax.broadcasted_iota(jnp.int32, (c, w), 1) // HG_DK
    subs = range(nsub - 1) if rev else range(1, nsub)
    atts = []
    for s in subs:
        lo, hi = c * s, c * (s + 1)
        beta = b2[hi:hi + 1] if rev else b2[lo - 1:lo]
        qt = q[lo:hi] * jnp.exp2(b2[lo:hi] - beta)
        kt = jnp.exp2((beta - kb) + keys_ref[s])
        qh = jnp.where(row_head == lane_head_q, jnp.concatenate([qt] * HG_HEADS, axis=0), 0.0)
        atts.append(_dot_nt(_mm(qh), _mm(kt)))
    r = _dot(_mm(jnp.concatenate(atts, axis=0)), vmm)
    zero = jnp.zeros((c, w), F32)
    outs = []
    for n, s in enumerate(subs):
        o = zero
        for h in range(HG_HEADS):
            row = (n * HG_HEADS + h) * c
            o = o + jnp.where(lane_head == h, r[row:row + c], 0.0)
        outs.append(o)
    acc = acc + jnp.concatenate(outs + [zero] if rev else [zero] + outs, axis=0)
    ps = [_mm(q * jnp.exp2(lk))]
    for d in range(1, c):
        shift = C - d if rev else d
        ps.append(_mm(q * jnp.exp2((b2 - pltpu.roll(kb, shift, 0)) + diag_ref[d])))
    a = _dot(jnp.concatenate(ps, axis=0), hsum)
    acc = acc + a[0:C] * v
    for d in range(1, c):
        acc = acc + a[d * C:(d + 1) * C] * pltpu.roll(v, C - d if rev else d, 0)
    return acc


def _hgrn_body(qf_ref, gf_ref, lf_ref, vf_ref, qb_ref, gb_ref, lbk_ref, vb_ref, kf_ref, df_ref, kb_ref, db_ref,
               hm_ref, hs_ref, of_ref, ob_ref, stf_ref, stb_ref):
    @pl.when(pl.program_id(1) == 0)
    def _reset():
        stf_ref[...] = jnp.zeros_like(stf_ref)
        stb_ref[...] = jnp.zeros_like(stb_ref)

    consts = (hm_ref[...], hs_ref[...])
    for i in range(qf_ref.shape[0]):
        of_ref[i] = _hgrn_chunk(qf_ref[i], gf_ref[i], lf_ref[i], vf_ref[i], kf_ref, df_ref, *consts, stf_ref.at[i], False)
        ob_ref[i] = _hgrn_chunk(qb_ref[i], gb_ref[i], lbk_ref[i], vb_ref[i], kb_ref, db_ref, *consts, stb_ref.at[i], True)


def _hgrn_scan(hg, bsz, n_tok):
    l_pad = n_tok + BLK
    nb = l_pad // BLK
    m = nb - 1
    fwd = lambda n: (n + m) % nb
    bwd = lambda n: jnp.where(n == m, m, m - 1 - n)
    hb = max(d for d in range(1, HG_BATCH + 1) if bsz % d == 0)
    spec = lambda order, col: pl.BlockSpec((hb, BLK, HG_WIDTH), lambda b, n: (b, order(n), col))
    masks = _hgrn_masks(False) + _hgrn_masks(True) + (_head_block_mask(F32), _head_block_mask(MM_DTYPE))
    return pl.pallas_call(
        _hgrn_body,
        grid=(bsz // hb, nb),
        in_specs=[spec(fwd, 0), spec(fwd, 1), spec(fwd, 5), spec(fwd, 3),
                  spec(bwd, 0), spec(bwd, 2), spec(bwd, 6), spec(bwd, 3)] + [_const_spec(t.shape) for t in masks],
        out_specs=[spec(fwd, 0), spec(bwd, 0)],
        out_shape=[jax.ShapeDtypeStruct((bsz, l_pad, HG_WIDTH), F32)] * 2,
        scratch_shapes=[pltpu.VMEM((hb, HG_WIDTH, HG_WIDTH), F32)] * 2,
        compiler_params=_params(("parallel", "arbitrary")),
        name="hgrn2_scan",
    )(hg, hg, hg, hg, hg, hg, hg, hg, *masks)


def _out_ffn_body(h_ref, ys_ref, ua_ref, att_ref, of_ref, ob_ref, gc_ref, gt_ref,
                  d_ref, wglu_ref, on_ref, hmean_ref, wa_ref, wb_ref, wc_ref, wo_ref, n2_ref, wg_ref, wu_ref, wd_ref,
                  out_ref, *, blocks_per_seq):
    tm = h_ref.shape[0]
    y = jax.nn.gelu(ys_ref[...] + d_ref[...] * ua_ref[...])
    y = y * _sigmoid(_dot(_mm(y), wglu_ref[...]))
    o = of_ref[...] + ob_ref[...]
    sq = o * o
    sq_hi = _mm(sq)
    ms = _dot(sq_hi, hmean_ref[...]) + _dot(_mm(sq - sq_hi.astype(F32)), hmean_ref[...])
    o = o * lax.rsqrt(ms + EPS) * on_ref[...]
    gc = gc_ref[...]
    o = o * (gc * _sigmoid(gc))
    mix = (_sigmoid(gt_ref[:, 0:D_MODEL]) * _dot(_mm(y), wa_ref[...])
           + _sigmoid(gt_ref[:, D_MODEL:2 * D_MODEL]) * _dot(att_ref[...], wb_ref[...])
           + _sigmoid(gt_ref[:, 2 * D_MODEL:3 * D_MODEL]) * _dot(_mm(o), wc_ref[...]))
    h = h_ref[...] + _dot(_mm(mix), wo_ref[...])
    hn = _mm(h * lax.rsqrt(jnp.mean(h * h, axis=-1, keepdims=True) + EPS) * n2_ref[...])
    for c0 in range(0, FFN_HIDDEN, FFN_CHUNK):
        gate = _dot(hn, wg_ref[:, c0:c0 + FFN_CHUNK])
        up = _dot(hn, wu_ref[:, c0:c0 + FFN_CHUNK])
        h = h + _dot(_mm(gate * _sigmoid(gate) * up), wd_ref[c0:c0 + FFN_CHUNK, :])
    out_ref[...] = h
    first_blk = pl.program_id(0) * (tm // BLK)
    for j in range(tm // BLK):
        @pl.when((first_blk + j) % blocks_per_seq == blocks_per_seq - 1)
        def _zero_rows():
            out_ref[j * BLK:j * BLK + PAD, :] = jnp.zeros((PAD, D_MODEL), F32)


def _out_ffn(h2d, ys, ua, att, o_f, o_b, hg, gates, lw, blocks_per_seq):
    rows = h2d.shape[0]
    tm = ROW_TILE_OUT
    row = lambda width, col=0: pl.BlockSpec((tm, width), lambda i: (i, col))
    weights = [lw["d"], lw["w_glu"], lw["onorm"], _head_block_mask(MM_DTYPE, 1.0 / HG_DK),
               lw["w_up_a"], lw["w_up_b"], lw["w_up_c"], lw["w_o"],
               lw["norm2"], lw["w_ffn_gate"], lw["w_ffn_up"], lw["w_ffn_down"]]
    return pl.pallas_call(
        functools.partial(_out_ffn_body, blocks_per_seq=blocks_per_seq),
        grid=(pl.cdiv(rows, tm),),
        in_specs=[row(D_MODEL), row(S5_WIDTH), row(S5_WIDTH), row(NA_WIDTH), row(HG_WIDTH), row(HG_WIDTH),
                  row(HG_WIDTH, 4), row(GATE_W)] + [_const_spec(w.shape) for w in weights],
        out_specs=row(D_MODEL),
        out_shape=jax.ShapeDtypeStruct((rows, D_MODEL), F32),
        compiler_params=_params(("parallel",)),
        name="mix_out_ffn",
    )(h2d, ys, ua, att, o_f, o_b, hg, gates, *weights)


def _final_norm_body(h_ref, g_ref, o_ref):
    x = h_ref[0]
    o_ref[0] = x * lax.rsqrt(jnp.mean(x * x, axis=-1, keepdims=True) + EPS) * g_ref[...]


def _final_norm(h, g, n_tok):
    bsz = h.shape[0]
    tl = 512
    spec = pl.BlockSpec((1, tl, D_MODEL), lambda b, j: (b, j, 0))
    return pl.pallas_call(
        _final_norm_body,
        grid=(bsz, n_tok // tl),
        in_specs=[spec, pl.BlockSpec((1, D_MODEL), lambda b, j: (0, 0))],
        out_specs=spec,
        out_shape=jax.ShapeDtypeStruct((bsz, n_tok, D_MODEL), F32),
        compiler_params=_params(("parallel", "parallel")),
        name="final_norm",
    )(h, g)


def _trunk(x, meta_tokens, layers, s5_ops, na_bias, final_g):
    bsz, n_tok, _ = x.shape
    l_pad = n_tok + BLK
    rows = bsz * l_pad
    meta = jnp.broadcast_to(meta_tokens[None].astype(F32), (bsz, N_META, D_MODEL))
    h = jnp.concatenate([x.astype(F32), jnp.zeros((bsz, PAD, D_MODEL), F32), meta], axis=1).reshape(rows, D_MODEL)
    for l, lw in enumerate(layers):
        ua, qkv, hg, gates = _in_proj(h, lw["norm1"], lw["w_in"], lw["lb"])
        ys = _s5_scan(ua, s5_ops, l, bsz, n_tok)
        att = _neighborhood_attention(qkv.reshape(bsz, l_pad, QKV_W), na_bias, l, bsz, n_tok)
        o_f, o_b = _hgrn_scan(hg.reshape(bsz, l_pad, HG_IN_W), bsz, n_tok)
        h = _out_ffn(h, ys, ua, att.reshape(rows, NA_WIDTH), o_f.reshape(rows, HG_WIDTH),
                     o_b.reshape(rows, HG_WIDTH), hg, gates, lw, l_pad // BLK)
    return _final_norm(h.reshape(bsz, l_pad, D_MODEL), final_g, n_tok)


def kernel(x_prompt, x_sample, meta_tokens, norm1_g, w_in, s5_a_re, s5_a_im, s5_log_dt, s5_b_re, s5_b_im,
           s5_c_re, s5_c_im, s5_d, s5_w_glu, na_rpb, hg_lb_logits, hg_onorm_g, w_up_a, w_up_b, w_up_c,
           w_o, norm2_g, w_ffn_gate, w_ffn_up, w_ffn_down, final_norm_g):
    depth = w_in.shape[0]
    sm = jax.nn.softmax(hg_lb_logits.astype(F32), axis=0)
    lbs = jnp.cumsum(sm, axis=0) - sm[0:1]
    layers = []
    for l in range(depth):
        layers.append(dict(
            norm1=norm1_g[l].astype(F32)[None],
            w_in=_mm(w_in[l]),
            lb=lbs[l][None],
            d=s5_d[l].astype(F32)[None],
            w_glu=_mm(s5_w_glu[l]),
            onorm=jnp.tile(hg_onorm_g[l].astype(F32), HG_HEADS)[None],
            w_up_a=_mm(w_up_a[l]), w_up_b=_mm(w_up_b[l]), w_up_c=_mm(w_up_c[l]), w_o=_mm(w_o[l]),
            norm2=norm2_g[l].astype(F32)[None],
            w_ffn_gate=_mm(w_ffn_gate[l]), w_ffn_up=_mm(w_ffn_up[l]), w_ffn_down=_mm(w_ffn_down[l]),
        ))
    s5_ops = jax.vmap(_s5_operators)(*(t.astype(F32) for t in (s5_a_re, s5_a_im, s5_log_dt, s5_b_re, s5_b_im,
                                                               s5_c_re, s5_c_im)))
    na_bias = jax.vmap(_na_bias)(na_rpb)
    final_g = final_norm_g.astype(F32)[None]
    return tuple(_trunk(x, meta_tokens, layers, s5_ops, na_bias, final_g) for x in (x_prompt, x_sample))
```

```python
import functools

import numpy as np
import jax
import jax.numpy as jnp
from jax import lax
from jax.experimental import pallas as pl
from jax.experimental.pallas import tpu as pltpu

F32 = jnp.float32
MM_DTYPE = jnp.bfloat16

D_MODEL = 1024
N_META = 16
GRID_W = 64
EPS = 1e-6
NEG = -1e30
BLK = 64
PAD = BLK - N_META
S5_WIDTH = 256
S5_GROUP = 16
S5_GROUPS = 16
S5_STATE = 64
S5_T = 64
S5_PACK_CHUNKS = 64
NA_HEADS = 8
NA_HEAD_DIM = 64
NA_WIDTH = 512
NA_ROWS = 8
NA_COLS = 16
NA_HG = 4
NA_QROWS = 32
HG_HEADS = 4
HG_DK = 64
HG_WIDTH = 256
HG_SUB = 8
HG_BATCH = 8
FFN_HIDDEN = 2816
FFN_CHUNK = 2816
IN_WIDTH = 6144
QKV_W = 3 * NA_WIDTH
HG_IN_W = 7 * HG_WIDTH
GATE_W = 3 * D_MODEL
ROW_TILE_IN = 256
ROW_TILE_OUT = 512
VMEM_LIMIT = 56 * 1024 * 1024


def _dot(a, b):
    return jnp.dot(a, b, preferred_element_type=F32)


def _dot_nt(a, b):
    return lax.dot_general(a, b, (((1,), (1,)), ((), ())), preferred_element_type=F32)


def _dot_tn(a, b):
    return lax.dot_general(a, b, (((0,), (0,)), ((), ())), preferred_element_type=F32)


def _dot_f32(a, b):
    return jnp.dot(a, b, preferred_element_type=F32, precision=lax.Precision.HIGHEST)


def _mm(x):
    return x.astype(MM_DTYPE)


def _sigmoid(x):
    return 0.5 * jnp.tanh(0.5 * x) + 0.5


def _head_block_mask(dtype, scale=1.0):
    r = np.arange(HG_WIDTH) // HG_DK
    return jnp.asarray(np.where(r[:, None] == r[None, :], scale, 0.0), dtype)


def _const_spec(shape):
    nd = len(shape)
    return pl.BlockSpec(shape, lambda *_: (0,) * nd, pipeline_mode=pl.Buffered(1))


def _params(sem):
    return pltpu.CompilerParams(dimension_semantics=sem, vmem_limit_bytes=VMEM_LIMIT)


def _in_proj_body(h_ref, g_ref, w_ref, lb_ref, ua_ref, qkv_ref, hg_ref, gt_ref):
    x = h_ref[...]
    xn = x * lax.rsqrt(jnp.mean(x * x, axis=-1, keepdims=True) + EPS) * g_ref[...]
    xn = _mm(xn)
    w_hg = HG_WIDTH
    c_qkv = S5_WIDTH
    c_hg = c_qkv + QKV_W
    c_gt = c_hg + 5 * w_hg
    z = _dot(xn, w_ref[:, c_hg:c_gt])
    q = z[:, 0:w_hg]
    hg_ref[:, 0:w_hg] = q * _sigmoid(q)
    hg_ref[:, 3 * w_hg:5 * w_hg] = z[:, 3 * w_hg:5 * w_hg]
    lb = lb_ref[...]
    for src, dst in ((1, 5), (2, 6)):
        sig = jax.nn.sigmoid(z[:, src * w_hg:(src + 1) * w_hg])
        hg_ref[:, src * w_hg:(src + 1) * w_hg] = jnp.log2(lb + (1.0 - lb) * sig)
        hg_ref[:, dst * w_hg:(dst + 1) * w_hg] = jnp.log2((1.0 - lb) * (1.0 - sig))
    ua_ref[...] = _dot(xn, w_ref[:, 0:c_qkv])
    qkv_ref[...] = _dot(xn, w_ref[:, c_qkv:c_hg]).astype(qkv_ref.dtype)
    gt_ref[...] = _dot(xn, w_ref[:, c_gt:c_gt + GATE_W])


def _in_proj(h2d, g, w, lb):
    rows = h2d.shape[0]
    tm = ROW_TILE_IN
    row = lambda width: pl.BlockSpec((tm, width), lambda i: (i, 0))
    return pl.pallas_call(
        _in_proj_body,
        grid=(pl.cdiv(rows, tm),),
        in_specs=[row(D_MODEL), _const_spec((1, D_MODEL)), _const_spec((D_MODEL, IN_WIDTH)), _const_spec((1, HG_WIDTH))],
        out_specs=[row(S5_WIDTH), row(QKV_W), row(HG_IN_W), row(GATE_W)],
        out_shape=[
            jax.ShapeDtypeStruct((rows, S5_WIDTH), F32),
            jax.ShapeDtypeStruct((rows, QKV_W), MM_DTYPE),
            jax.ShapeDtypeStruct((rows, HG_IN_W), F32),
            jax.ShapeDtypeStruct((rows, GATE_W), F32),
        ],
        compiler_params=_params(("parallel",)),
        name="in_proj",
    )(h2d, g, w, lb)


def _s5_operators(a_re, a_im, log_dt, b_re, b_im, c_re, c_im):
    T, G, P, C = S5_T, S5_GROUPS, S5_STATE, S5_GROUP
    hi = lax.Precision.HIGHEST
    dt = jnp.exp(log_dt)[..., None]
    mag = jnp.exp(a_re * dt)
    lr = mag * jnp.cos(a_im * dt)
    li = mag * jnp.sin(a_im * dt)
    den = a_re * a_re + a_im * a_im
    nr = lr - 1.0
    z_re = (nr * a_re + li * a_im) / den
    z_im = (li * a_re - nr * a_im) / den
    bb_re = z_re[..., None] * b_re - z_im[..., None] * b_im
    bb_im = z_re[..., None] * b_im + z_im[..., None] * b_re
    d = jnp.arange(T + 1, dtype=F32)
    pmag = jnp.exp((a_re * dt)[..., None] * d)
    pr = pmag * jnp.cos((a_im * dt)[..., None] * d)
    pi = pmag * jnp.sin((a_im * dt)[..., None] * d)
    lb_re = pr[..., None] * bb_re[:, :, :, None, :] - pi[..., None] * bb_im[:, :, :, None, :]
    lb_im = pr[..., None] * bb_im[:, :, :, None, :] + pi[..., None] * bb_re[:, :, :, None, :]
    cl_re = c_re[..., None] * pr[:, :, None] - c_im[..., None] * pi[:, :, None]
    cl_im = c_re[..., None] * pi[:, :, None] + c_im[..., None] * pr[:, :, None]
    kk = (jnp.einsum('xgop,xgpdi->xgdoi', c_re, lb_re, precision=hi)
          - jnp.einsum('xgop,xgpdi->xgdoi', c_im, lb_im, precision=hi))[:, :, :T]
    kf, kb = kk[0], kk[1]
    kall = jnp.concatenate([kb[:, :0:-1], (kf[:, :1] + kb[:, :1]), kf[:, 1:]], axis=1)
    kall_t = kall.transpose(0, 3, 1, 2).reshape(G, C, (2 * T - 1) * C)

    def pack_cols(re, im):
        z = jnp.zeros_like(re)
        odd = (jnp.arange(G) % 2 == 1)[:, None, None]
        return jnp.concatenate([jnp.where(odd, z, re), jnp.where(odd, re, z),
                                jnp.where(odd, z, im), jnp.where(odd, im, z)], axis=-1)

    vf_re = lb_re[0][:, :, T - 1::-1][:, :, :T].transpose(0, 2, 3, 1).reshape(G, T * C, P)
    vf_im = lb_im[0][:, :, T - 1::-1][:, :, :T].transpose(0, 2, 3, 1).reshape(G, T * C, P)
    vb_re = lb_re[1][:, :, :T].transpose(0, 2, 3, 1).reshape(G, T * C, P)
    vb_im = lb_im[1][:, :, :T].transpose(0, 2, 3, 1).reshape(G, T * C, P)
    v_f = pack_cols(vf_re, vf_im)
    v_b = pack_cols(vb_re, vb_im)
    wf_re = cl_re[0][..., 1:T + 1].transpose(0, 2, 3, 1).reshape(G, P, T * C)
    wf_im = cl_im[0][..., 1:T + 1].transpose(0, 2, 3, 1).reshape(G, P, T * C)
    wb_re = cl_re[1][..., T:0:-1].transpose(0, 2, 3, 1).reshape(G, P, T * C)
    wb_im = cl_im[1][..., T:0:-1].transpose(0, 2, 3, 1).reshape(G, P, T * C)
    pack_rows = lambda re, im: pack_cols(re.transpose(0, 2, 1), -im.transpose(0, 2, 1)).transpose(0, 2, 1)
    w_f = pack_rows(wf_re, wf_im)
    w_b = pack_rows(wb_re, wb_im)
    lam = jnp.stack([pr[0, :, :, T], pi[0, :, :, T], pr[1, :, :, T], pi[1, :, :, T]], axis=0)
    lam = lam.reshape(4, G // 2, 2 * P).transpose(1, 0, 2)
    return kall_t, _mm(v_f), _mm(v_b), _mm(w_f), _mm(w_b), lam


def _s5_pack_body(u_ref, o_ref):
    nbb = o_ref.shape[1]
    per_vreg = 128 // S5_GROUP
    slot = lax.broadcasted_iota(jnp.int32, (nbb, 128), 1) // S5_GROUP
    for j in range(S5_T // per_vreg):
        slabs = [u_ref[pl.ds(j * per_vreg + t, nbb, stride=S5_T), :] for t in range(per_vreg)]
        slabs = [s if t == 0 else pltpu.roll(s, t * S5_GROUP, 1) for t, s in enumerate(slabs)]
        for g in range(per_vreg):
            blk = slabs[0]
            for t in range(1, per_vreg):
                blk = jnp.where(slot == (t + g) % per_vreg, slabs[t], blk)
            o_ref[g, :, j * 128:(j + 1) * 128] = blk.astype(o_ref.dtype)


def _s5_unpack_body(y_ref, o_ref):
    nbb = y_ref.shape[1]
    per_vreg = 128 // S5_GROUP
    slot = lax.broadcasted_iota(jnp.int32, (nbb, 128), 1) // S5_GROUP
    for j in range(S5_T // per_vreg):
        blks = [y_ref[g, :, j * 128:(j + 1) * 128] for g in range(per_vreg)]
        for t in range(per_vreg):
            slab = blks[0]
            for g in range(1, per_vreg):
                slab = jnp.where(slot == (t + g) % per_vreg, blks[g], slab)
            if t:
                slab = pltpu.roll(slab, 128 - t * S5_GROUP, 1)
            o_ref[pl.ds(j * per_vreg + t, nbb, stride=S5_T), :] = slab


def _s5_slot(s, g):
    n = 128 // S5_GROUP
    return (s // n) * n + (s % n + g) % n


def _s5_inc_body(u_ref, vf_ref, vb_ref, incf_ref, incb_ref, vs_ref):
    for k in range(2):
        g = 2 * pl.program_id(0) + k
        for s in range(S5_T):
            row0 = pl.multiple_of(_s5_slot(s, g) * S5_GROUP, S5_GROUP)
            vs_ref[k, 0, pl.ds(row0, S5_GROUP), :] = vf_ref[k, s * S5_GROUP:(s + 1) * S5_GROUP, :]
            vs_ref[k, 1, pl.ds(row0, S5_GROUP), :] = vb_ref[k, s * S5_GROUP:(s + 1) * S5_GROUP, :]
    incf_ref[...] = _dot(u_ref[0], vs_ref[0, 0]) + _dot(u_ref[1], vs_ref[1, 0])
    incb_ref[...] = _dot(u_ref[0], vs_ref[0, 1]) + _dot(u_ref[1], vs_ref[1, 1])


def _s5_scan_body(ifr_ref, ifi_ref, ibr_ref, ibi_ref, lam_ref, xfr_ref, xfi_ref, xbr_ref, xbi_ref, *, bsz, nc):
    lam = lam_ref[0]
    zero = jnp.zeros((bsz, 2 * S5_STATE), F32)
    m = nc - 1

    def run(inc_re, inc_im, x_re, x_im, lr, li, reverse):
        def step(i, carry):
            if reverse:
                j = jnp.where(i == m, m, m - 1 - i)
            else:
                j = jnp.where(i == 0, m, i - 1)
            rows = pl.ds(j, bsz, stride=nc)
            xr, xi = carry
            x_re[rows, :] = xr
            x_im[rows, :] = xi
            return (lr * xr - li * xi + inc_re[rows, :], lr * xi + li * xr + inc_im[rows, :])

        lax.fori_loop(0, nc, step, (zero, zero))

    run(ifr_ref, ifi_ref, xfr_ref, xfi_ref, lam[0:1], lam[1:2], False)
    run(ibr_ref, ibi_ref, xbr_ref, xbi_ref, lam[2:3], lam[3:4], True)


def _s5_out_body(u_ref, kall_ref, wf_ref, wb_ref, xfr_ref, xfi_ref, xbr_ref, xbi_ref, y_ref, m_ref):
    xf = _mm(jnp.concatenate([xfr_ref[...], xfi_ref[...]], axis=1))
    xb = _mm(jnp.concatenate([xbr_ref[...], xbi_ref[...]], axis=1))
    tw = S5_T * S5_GROUP
    per_vreg = 128 // S5_GROUP
    for k in range(2):
        kall = kall_ref[k]
        g = 2 * pl.program_id(0) + k
        for s in range(S5_T):
            off = (S5_T - 1 - s) * S5_GROUP
            row0 = pl.multiple_of(_s5_slot(s, g) * S5_GROUP, S5_GROUP)
            m_ref[k, pl.ds(row0, S5_GROUP), :] = kall[:, off:off + tw].astype(m_ref.dtype)
        y = _dot(u_ref[k], m_ref[k]) + _dot(xf, wf_ref[k]) + _dot(xb, wb_ref[k])
        shift = (g % per_vreg) * S5_GROUP
        for b in range(tw // 128):
            y_ref[k, :, b * 128:(b + 1) * 128] = pltpu.roll(y[:, b * 128:(b + 1) * 128], shift, 1)


def _s5_scan(u_a, ops, layer, bsz, n_tok):
    kall_t, v_f, v_b, w_f, w_b, lam = ops
    T, G, C = S5_T, S5_GROUPS, S5_GROUP
    l_pad = n_tok + BLK
    nc = l_pad // T
    rows = nc * bsz
    tw = T * C
    sw = 4 * S5_STATE
    nbb = S5_PACK_CHUNKS
    gh = 128 // C
    tok_half = pl.BlockSpec((nbb * T, 128), lambda i, h: (i, h))
    grp_half = pl.BlockSpec((gh, nbb, tw), lambda i, h: (h, i, 0))
    u = pl.pallas_call(
        _s5_pack_body,
        grid=(pl.cdiv(rows, nbb), G // gh),
        in_specs=[tok_half],
        out_specs=grp_half,
        out_shape=jax.ShapeDtypeStruct((G, rows, tw), MM_DTYPE),
        compiler_params=_params(("parallel", "parallel")),
        name="s5_pack",
    )(u_a)
    pair3 = lambda r, c: pl.BlockSpec((2, r, c), lambda p: (p, 0, 0))
    op3 = lambda r, c: pl.BlockSpec((None, 2, r, c), lambda p: (layer, p, 0, 0))
    lane = lambda r: pl.BlockSpec((r, sw), lambda p: (0, p))
    re = pl.BlockSpec((rows, sw // 2), lambda p: (0, 2 * p))
    im = pl.BlockSpec((rows, sw // 2), lambda p: (0, 2 * p + 1))
    half = pl.BlockSpec((rows, sw // 2), lambda p: (0, p))
    inc_f, inc_b = pl.pallas_call(
        _s5_inc_body,
        grid=(G // 2,),
        in_specs=[pair3(rows, tw), op3(tw, sw), op3(tw, sw)],
        out_specs=[lane(rows), lane(rows)],
        out_shape=[jax.ShapeDtypeStruct((rows, sw * G // 2), F32)] * 2,
        scratch_shapes=[pltpu.VMEM((2, 2, tw, sw), MM_DTYPE)],
        compiler_params=_params(("parallel",)),
        name="s5_inc",
    )(u, v_f, v_b)
    states = pl.pallas_call(
        functools.partial(_s5_scan_body, bsz=bsz, nc=nc),
        grid=(G // 2,),
        in_specs=[re, im, re, im, pl.BlockSpec((None, 1, 4, 2 * S5_STATE), lambda p: (layer, p, 0, 0))],
        out_specs=[half] * 4,
        out_shape=[jax.ShapeDtypeStruct((rows, sw * G // 4), F32)] * 4,
        compiler_params=_params(("parallel",)),
        name="s5_chunk_scan",
    )(inc_f, inc_f, inc_b, inc_b, lam)
    y = pl.pallas_call(
        _s5_out_body,
        grid=(G // 2,),
        in_specs=[pair3(rows, tw), op3(C, (2 * T - 1) * C), op3(sw, tw), op3(sw, tw)] + [half] * 4,
        out_specs=pair3(rows, tw),
        out_shape=jax.ShapeDtypeStruct((G, rows, tw), F32),
        scratch_shapes=[pltpu.VMEM((2, tw, tw), MM_DTYPE)],
        compiler_params=_params(("parallel",)),
        name="s5_out",
    )(u, kall_t, w_f, w_b, *states)
    return pl.pallas_call(
        _s5_unpack_body,
        grid=(pl.cdiv(rows, nbb), G // gh),
        in_specs=[grp_half],
        out_specs=tok_half,
        out_shape=jax.ShapeDtypeStruct((rows * T, S5_WIDTH), F32),
        compiler_params=_params(("parallel", "parallel")),
        name="s5_unpack",
    )(y)


def _na_bias(rpb):
    cols = np.arange(GRID_W)
    wstart = np.clip(cols - NA_COLS // 2, 0, GRID_W - NA_COLS)
    inwin = (cols[None, :] >= wstart[:, None]) & (cols[None, :] < wstart[:, None] + NA_COLS)
    dcol = np.clip(cols[None, :] - cols[:, None], -(NA_COLS - 1), NA_COLS - 1) + NA_COLS - 1
    drow = np.arange(NA_ROWS)[None, :] - np.arange(NA_ROWS)[:, None] + NA_ROWS - 1
    b = rpb.astype(F32)[:, drow][:, :, :, dcol]
    b = jnp.where(inwin[None, None, None], b, NEG)
    b = b.transpose(1, 0, 3, 2, 4)
    return b.reshape(NA_ROWS, NA_HEADS // NA_HG, NA_HG * GRID_W, NA_ROWS * GRID_W)


def _head_select(o, rows_per_head, lane_head):
    out = jnp.where(lane_head == 0, o[0:rows_per_head], 0.0)
    for h in range(1, NA_HG):
        out = out + jnp.where(lane_head == h, o[h * rows_per_head:(h + 1) * rows_per_head], 0.0)
    return out


def _na_body(q_ref, k_ref, v_ref, bias_ref, o_ref, *, n_tok):
    rg = pl.program_id(2)
    n_rg = n_tok // (NA_QROWS * GRID_W)
    grid_rows = n_tok // GRID_W
    meta0 = n_tok + PAD
    scale = NA_HEAD_DIM ** -0.5
    hw = NA_HG * NA_HEAD_DIM
    km = k_ref[0, meta0:meta0 + N_META, :]
    vm = v_ref[0, meta0:meta0 + N_META, :]

    def masked_queries(q, nq):
        row_head = lax.broadcasted_iota(jnp.int32, (NA_HG * nq, hw), 0) // nq
        lane_head = lax.broadcasted_iota(jnp.int32, (NA_HG * nq, hw), 1) // NA_HEAD_DIM
        qq = jnp.concatenate([q] * NA_HG, axis=0)
        return jnp.where(row_head == lane_head, qq, jnp.zeros_like(qq))

    @pl.when(rg < n_rg)
    def _grid_rows():
        lane_head = lax.broadcasted_iota(jnp.int32, (GRID_W, hw), 1) // NA_HEAD_DIM

        def row_step(rr, carry):
            r = rg * NA_QROWS + rr
            rs = jnp.clip(r - NA_ROWS // 2, 0, grid_rows - NA_ROWS)
            q0 = pl.multiple_of(rr * GRID_W, GRID_W)
            k0 = pl.multiple_of(rs * GRID_W, GRID_W)
            q = q_ref[0, pl.ds(q0, GRID_W), :] * scale
            qm = masked_queries(q.astype(q_ref.dtype), GRID_W)
            ks = k_ref[0, pl.ds(k0, NA_ROWS * GRID_W), :]
            vs = v_ref[0, pl.ds(k0, NA_ROWS * GRID_W), :]
            s = _dot_nt(qm, ks) + bias_ref[r - rs, 0]
            sm = _dot_nt(qm, km)
            fold = lambda x, op: functools.reduce(op, [x[:, i:i + 128] for i in range(0, x.shape[1], 128)])
            sm_wide = jnp.concatenate([sm, jnp.full((sm.shape[0], 128 - N_META), NEG, F32)], axis=1)
            mx = jnp.max(jnp.maximum(fold(s, jnp.maximum), sm_wide), axis=-1, keepdims=True)
            p = jnp.exp(s - mx)
            pm = jnp.exp(sm_wide - mx)
            den = jnp.sum(fold(p, jnp.add) + pm, axis=-1, keepdims=True)
            o = (_dot(_mm(p), vs) + _dot(_mm(pm[:, 0:N_META]), vm)) * (1.0 / den)
            o_ref[0, pl.ds(q0, GRID_W), :] = _head_select(o, GRID_W, lane_head).astype(o_ref.dtype)
            return carry

        lax.fori_loop(0, NA_QROWS, row_step, 0, unroll=True)

    @pl.when(rg == n_rg)
    def _meta_rows():
        lane_head = lax.broadcasted_iota(jnp.int32, (N_META, hw), 1) // NA_HEAD_DIM
        q = q_ref[0, PAD:PAD + N_META, :] * scale
        qm = masked_queries(q.astype(q_ref.dtype), N_META)
        s = _dot_nt(qm, km)
        p = jnp.exp(s - jnp.max(s, axis=-1, keepdims=True))
        o = _dot(_mm(p), vm) / jnp.sum(p, axis=-1, keepdims=True)
        o_ref[0, 0:PAD, :] = jnp.zeros((PAD, hw), o_ref.dtype)
        o_ref[0, PAD:BLK, :] = _head_select(o, N_META, lane_head).astype(o_ref.dtype)


def _neighborhood_attention(qkv, bias, layer, bsz, n_tok):
    l_pad = n_tok + BLK
    hw = NA_HG * NA_HEAD_DIM
    n_hg = NA_HEADS // NA_HG
    qrows = NA_QROWS * GRID_W
    whole = lambda off: pl.BlockSpec((1, l_pad, hw), lambda b, g, r: (b, 0, off + g), pipeline_mode=pl.Buffered(1))
    return pl.pallas_call(
        functools.partial(_na_body, n_tok=n_tok),
        grid=(bsz, n_hg, n_tok // qrows + 1),
        in_specs=[
            pl.BlockSpec((1, qrows, hw), lambda b, g, r: (b, r, g)),
            whole(n_hg),
            whole(2 * n_hg),
            pl.BlockSpec((None, NA_ROWS, 1, NA_HG * GRID_W, NA_ROWS * GRID_W), lambda b, g, r: (layer, 0, g, 0, 0)),
        ],
        out_specs=pl.BlockSpec((1, qrows, hw), lambda b, g, r: (b, r, g)),
        out_shape=jax.ShapeDtypeStruct((bsz, l_pad, NA_WIDTH), MM_DTYPE),
        compiler_params=_params(("parallel", "parallel", "arbitrary")),
        name="neighborhood_attention",
    )(qkv, qkv, qkv, bias)


def _hgrn_masks(rev):
    C, c = BLK, HG_SUB
    i = np.arange(C)
    s = np.arange(C // c)[:, None]
    d = np.arange(c)[:, None]
    keys_ok = (i[None] >= (s + 1) * c) if rev else (i[None] < s * c)
    diag_ok = (i[None] % c + d <= c - 1) if rev else (i[None] % c >= d)
    table = lambda ok: np.broadcast_to(np.where(ok, 0.0, NEG).astype(np.float32)[:, :, None], ok.shape + (HG_WIDTH,))
    return jnp.asarray(table(keys_ok)), jnp.asarray(table(diag_ok))


def _cumsum_rows(tri, x):
    hi = _mm(x)
    rest = x - hi.astype(F32)
    mid = _mm(rest)
    lo = _mm(rest - mid.astype(F32))
    w = x.shape[1]
    y = _dot(tri, jnp.concatenate([hi, mid, lo], axis=1))
    return y[:, 0:w] + y[:, w:2 * w] + y[:, 2 * w:3 * w]


def _hgrn_chunk(q, g2, lk, v, keys_ref, diag_ref, head_mask, hsum, st_ref, rev):
    C, c = BLK, HG_SUB
    nsub = C // c
    w = HG_WIDTH
    ri = lax.broadcasted_iota(jnp.int32, (C, C), 0)
    ci = lax.broadcasted_iota(jnp.int32, (C, C), 1)
    tri = jnp.where((ci >= ri) if rev else (ci <= ri), 1.0, 0.0).astype(MM_DTYPE)
    b2 = _cumsum_rows(tri, g2)
    kb = b2 - lk
    edge = b2[0:1] if rev else b2[C - 1:C]
    st = st_ref[...]
    acc = _dot_nt(_mm(q * jnp.exp2(b2)), _mm(st))
    d_t = _dot_tn(_mm(v), _mm(jnp.exp2(edge - kb)))
    st_ref[...] = st * jnp.exp2(edge) + d_t * head_mask
    vmm = _mm(v)
    row_head = lax.broadcasted_iota(jnp.int32, (HG_HEADS * c, w), 0) // c
    lane_head_q = lax.broadcasted_iota(jnp.int32, (HG_HEADS * c, w), 1) // HG_DK
    lane_head = lax.broadcasted_iota(jnp.int32, (c, w), 1) // HG_DK
    subs = range(nsub - 1) if rev else range(1, nsub)
    atts = []
    for s in subs:
        lo, hi = c * s, c * (s + 1)
        beta = b2[hi:hi + 1] if rev else b2[lo - 1:lo]
        qt = q[lo:hi] * jnp.exp2(b2[lo:hi] - beta)
        kt = jnp.exp2((beta - kb) + keys_ref[s])
        qh = jnp.where(row_head == lane_head_q, jnp.concatenate([qt] * HG_HEADS, axis=0), 0.0)
        atts.append(_dot_nt(_mm(qh), _mm(kt)))
    r = _dot(_mm(jnp.concatenate(atts, axis=0)), vmm)
    zero = jnp.zeros((c, w), F32)
    outs = []
    for n, s in enumerate(subs):
        o = zero
        for h in range(HG_HEADS):
            row = (n * HG_HEADS + h) * c
            o = o + jnp.where(lane_head == h, r[row:row + c], 0.0)
        outs.append(o)
    acc = acc + jnp.concatenate(outs + [zero] if rev else [zero] + outs, axis=0)
    ps = [_mm(q * jnp.exp2(lk))]
    for d in range(1, c):
        shift = C - d if rev else d
        ps.append(_mm(q * jnp.exp2((b2 - pltpu.roll(kb, shift, 0)) + diag_ref[d])))
    a = _dot(jnp.concatenate(ps, axis=0), hsum)
    acc = acc + a[0:C] * v
    for d in range(1, c):
        acc = acc + a[d * C:(d + 1) * C] * pltpu.roll(v, C - d if rev else d, 0)
    return acc


def _hgrn_body(qf_ref, gf_ref, lf_ref, vf_ref, qb_ref, gb_ref, lbk_ref, vb_ref, kf_ref, df_ref, kb_ref, db_ref,
               hm_ref, hs_ref, of_ref, ob_ref, stf_ref, stb_ref):
    @pl.when(pl.program_id(1) == 0)
    def _reset():
        stf_ref[...] = jnp.zeros_like(stf_ref)
        stb_ref[...] = jnp.zeros_like(stb_ref)

    consts = (hm_ref[...], hs_ref[...])
    for i in range(qf_ref.shape[0]):
        of_ref[i] = _hgrn_chunk(qf_ref[i], gf_ref[i], lf_ref[i], vf_ref[i], kf_ref, df_ref, *consts, stf_ref.at[i], False)
        ob_ref[i] = _hgrn_chunk(qb_ref[i], gb_ref[i], lbk_ref[i], vb_ref[i], kb_ref, db_ref, *consts, stb_ref.at[i], True)


def _hgrn_scan(hg, bsz, n_tok):
    l_pad = n_tok + BLK
    nb = l_pad // BLK
    m = nb - 1
    fwd = lambda n: (n + m) % nb
    bwd = lambda n: jnp.where(n == m, m, m - 1 - n)
    hb = max(d for d in range(1, HG_BATCH + 1) if bsz % d == 0)
    spec = lambda order, col: pl.BlockSpec((hb, BLK, HG_WIDTH), lambda b, n: (b, order(n), col))
    masks = _hgrn_masks(False) + _hgrn_masks(True) + (_head_block_mask(F32), _head_block_mask(MM_DTYPE))
    return pl.pallas_call(
        _hgrn_body,
        grid=(bsz // hb, nb),
        in_specs=[spec(fwd, 0), spec(fwd, 1), spec(fwd, 5), spec(fwd, 3),
                  spec(bwd, 0), spec(bwd, 2), spec(bwd, 6), spec(bwd, 3)] + [_const_spec(t.shape) for t in masks],
        out_specs=[spec(fwd, 0), spec(bwd, 0)],
        out_shape=[jax.ShapeDtypeStruct((bsz, l_pad, HG_WIDTH), F32)] * 2,
        scratch_shapes=[pltpu.VMEM((hb, HG_WIDTH, HG_WIDTH), F32)] * 2,
        compiler_params=_params(("parallel", "arbitrary")),
        name="hgrn2_scan",
    )(hg, hg, hg, hg, hg, hg, hg, hg, *masks)


def _out_ffn_body(h_ref, ys_ref, ua_ref, att_ref, of_ref, ob_ref, gc_ref, gt_ref,
                  d_ref, wglu_ref, on_ref, hmean_ref, wa_ref, wb_ref, wc_ref, wo_ref, n2_ref, wg_ref, wu_ref, wd_ref,
                  out_ref, *, blocks_per_seq):
    tm = h_ref.shape[0]
    y = jax.nn.gelu(ys_ref[...] + d_ref[...] * ua_ref[...])
    y = y * _sigmoid(_dot(_mm(y), wglu_ref[...]))
    o = of_ref[...] + ob_ref[...]
    sq = o * o
    sq_hi = _mm(sq)
    ms = _dot(sq_hi, hmean_ref[...]) + _dot(_mm(sq - sq_hi.astype(F32)), hmean_ref[...])
    o = o * lax.rsqrt(ms + EPS) * on_ref[...]
    gc = gc_ref[...]
    o = o * (gc * _sigmoid(gc))
    mix = (_sigmoid(gt_ref[:, 0:D_MODEL]) * _dot(_mm(y), wa_ref[...])
           + _sigmoid(gt_ref[:, D_MODEL:2 * D_MODEL]) * _dot(att_ref[...], wb_ref[...])
           + _sigmoid(gt_ref[:, 2 * D_MODEL:3 * D_MODEL]) * _dot(_mm(o), wc_ref[...]))
    h = h_ref[...] + _dot(_mm(mix), wo_ref[...])
    hn = _mm(h * lax.rsqrt(jnp.mean(h * h, axis=-1, keepdims=True) + EPS) * n2_ref[...])
    for c0 in range(0, FFN_HIDDEN, FFN_CHUNK):
        gate = _dot(hn, wg_ref[:, c0:c0 + FFN_CHUNK])
        up = _dot(hn, wu_ref[:, c0:c0 + FFN_CHUNK])
        h = h + _dot(_mm(gate * _sigmoid(gate) * up), wd_ref[c0:c0 + FFN_CHUNK, :])
    out_ref[...] = h
    first_blk = pl.program_id(0) * (tm // BLK)
    for j in range(tm // BLK):
        @pl.when((first_blk + j) % blocks_per_seq == blocks_per_seq - 1)
        def _zero_rows():
            out_ref[j * BLK:j * BLK + PAD, :] = jnp.zeros((PAD, D_MODEL), F32)


def _out_ffn(h2d, ys, ua, att, o_f, o_b, hg, gates, lw, blocks_per_seq):
    rows = h2d.shape[0]
    tm = ROW_TILE_OUT
    row = lambda width, col=0: pl.BlockSpec((tm, width), lambda i: (i, col))
    weights = [lw["d"], lw["w_glu"], lw["onorm"], _head_block_mask(MM_DTYPE, 1.0 / HG_DK),
               lw["w_up_a"], lw["w_up_b"], lw["w_up_c"], lw["w_o"],
               lw["norm2"], lw["w_ffn_gate"], lw["w_ffn_up"], lw["w_ffn_down"]]
    return pl.pallas_call(
        functools.partial(_out_ffn_body, blocks_per_seq=blocks_per_seq),
        grid=(pl.cdiv(rows, tm),),
        in_specs=[row(D_MODEL), row(S5_WIDTH), row(S5_WIDTH), row(NA_WIDTH), row(HG_WIDTH), row(HG_WIDTH),
                  row(HG_WIDTH, 4), row(GATE_W)] + [_const_spec(w.shape) for w in weights],
        out_specs=row(D_MODEL),
        out_shape=jax.ShapeDtypeStruct((rows, D_MODEL), F32),
        compiler_params=_params(("parallel",)),
        name="mix_out_ffn",
    )(h2d, ys, ua, att, o_f, o_b, hg, gates, *weights)


def _final_norm_body(h_ref, g_ref, o_ref):
    x = h_ref[0]
    o_ref[0] = x * lax.rsqrt(jnp.mean(x * x, axis=-1, keepdims=True) + EPS) * g_ref[...]


def _final_norm(h, g, n_tok):
    bsz = h.shape[0]
    tl = 512
    spec = pl.BlockSpec((1, tl, D_MODEL), lambda b, j: (b, j, 0))
    return pl.pallas_call(
        _final_norm_body,
        grid=(bsz, n_tok // tl),
        in_specs=[spec, pl.BlockSpec((1, D_MODEL), lambda b, j: (0, 0))],
        out_specs=spec,
        out_shape=jax.ShapeDtypeStruct((bsz, n_tok, D_MODEL), F32),
        compiler_params=_params(("parallel", "parallel")),
        name="final_norm",
    )(h, g)


def _trunk(x, meta_tokens, layers, s5_ops, na_bias, final_g):
    bsz, n_tok, _ = x.shape
    l_pad = n_tok + BLK
    rows = bsz * l_pad
    meta = jnp.broadcast_to(meta_tokens[None].astype(F32), (bsz, N_META, D_MODEL))
    h = jnp.concatenate([x.astype(F32), jnp.zeros((bsz, PAD, D_MODEL), F32), meta], axis=1).reshape(rows, D_MODEL)
    for l, lw in enumerate(layers):
        ua, qkv, hg, gates = _in_proj(h, lw["norm1"], lw["w_in"], lw["lb"])
        ys = _s5_scan(ua, s5_ops, l, bsz, n_tok)
        att = _neighborhood_attention(qkv.reshape(bsz, l_pad, QKV_W), na_bias, l, bsz, n_tok)
        o_f, o_b = _hgrn_scan(hg.reshape(bsz, l_pad, HG_IN_W), bsz, n_tok)
        h = _out_ffn(h, ys, ua, att.reshape(rows, NA_WIDTH), o_f.reshape(rows, HG_WIDTH),
                     o_b.reshape(rows, HG_WIDTH), hg, gates, lw, l_pad // BLK)
    return _final_norm(h.reshape(bsz, l_pad, D_MODEL), final_g, n_tok)


def kernel(x_prompt, x_sample, meta_tokens, norm1_g, w_in, s5_a_re, s5_a_im, s5_log_dt, s5_b_re, s5_b_im,
           s5_c_re, s5_c_im, s5_d, s5_w_glu, na_rpb, hg_lb_logits, hg_onorm_g, w_up_a, w_up_b, w_up_c,
           w_o, norm2_g, w_ffn_gate, w_ffn_up, w_ffn_down, final_norm_g):
    depth = w_in.shape[0]
    sm = jax.nn.softmax(hg_lb_logits.astype(F32), axis=0)
    lbs = jnp.cumsum(sm, axis=0) - sm[0:1]
    layers = []
    for l in range(depth):
        layers.append(dict(
            norm1=norm1_g[l].astype(F32)[None],
            w_in=_mm(w_in[l]),
            lb=lbs[l][None],
            d=s5_d[l].astype(F32)[None],
            w_glu=_mm(s5_w_glu[l]),
            onorm=jnp.tile(hg_onorm_g[l].astype(F32), HG_HEADS)[None],
            w_up_a=_mm(w_up_a[l]), w_up_b=_mm(w_up_b[l]), w_up_c=_mm(w_up_c[l]), w_o=_mm(w_o[l]),
            norm2=norm2_g[l].astype(F32)[None],
            w_ffn_gate=_mm(w_ffn_gate[l]), w_ffn_up=_mm(w_ffn_up[l]), w_ffn_down=_mm(w_ffn_down[l]),
        ))
    s5_ops = jax.vmap(_s5_operators)(*(t.astype(F32) for t in (s5_a_re, s5_a_im, s5_log_dt, s5_b_re, s5_b_im,
                                                               s5_c_re, s5_c_im)))
    na_bias = jax.vmap(_na_bias)(na_rpb)
    final_g = final_norm_g.astype(F32)[None]
    return tuple(_trunk(x, meta_tokens, layers, s5_ops, na_bias, final_g) for x in (x_prompt, x_sample))
```
